```python
import math
import jax
import jax.numpy as jnp
from jax import lax
import numpy as np

D_MODEL = 1024
BATCH = 4
SEQ = 4096
DEPTH = 2
DEC_BATCH = 32
DEC_SEQ = 8
PAST_LEN = 8192
PAGE_SIZE = 128

N_A_LAYERS = DEPTH // 2
N_B_LAYERS = DEPTH - N_A_LAYERS

GDN_DK = 128
GDN_DV = 128
GDN_QK_HEADS = D_MODEL // GDN_DK
GDN_V_HEADS = 2 * GDN_QK_HEADS
GDN_QK_WIDTH = GDN_QK_HEADS * GDN_DK
GDN_V_WIDTH = GDN_V_HEADS * GDN_DV
GDN_CONV_DIM = 2 * GDN_QK_WIDTH + GDN_V_WIDTH
GDN_IN_WIDTH = GDN_CONV_DIM + GDN_V_WIDTH + 2 * GDN_V_HEADS
CONV_WIDTH = 4
GDN_CHUNK = 64

DIFF_DH = 64
DIFF_HEADS = D_MODEL // (2 * DIFF_DH)
DIFF_VH = 2 * DIFF_DH
DIFF_WIDTH = DIFF_HEADS * DIFF_VH
ROPE_DIMS = DIFF_DH // 4
ROPE_THETA = 500000.0
Q_BLOCK = 128

NORM_EPS = 1e-6
F32 = jnp.float32

kernel_name = 'yoco_gated_delta_diff_attn_step'


def rms_norm(x, gain):
    xf = x.astype(F32)
    y = xf * lax.rsqrt(jnp.mean(xf * xf, axis=-1, keepdims=True) + NORM_EPS)
    return (y * gain.astype(F32)).astype(x.dtype)


def l2_normalize(x):
    return x * lax.rsqrt(jnp.sum(x * x, axis=-1, keepdims=True) + NORM_EPS)


def ada_split(c, w, b, n):
    return jnp.split(c @ w + b, n, axis=-1)


def modulate(x, gain, shift, scale):
    return rms_norm(x, gain) * (1 + scale[:, None, :]) + shift[:, None, :]


def rotary_tables(pos):
    inv_freq = ROPE_THETA ** (-jnp.arange(0, ROPE_DIMS, 2, dtype=F32) / ROPE_DIMS)
    ang = pos.astype(F32)[:, None] * inv_freq[None, :]
    return jnp.cos(ang), jnp.sin(ang)


def partial_rotary(x, cos, sin):
    half = ROPE_DIMS // 2
    c = cos[None, :, None, None, :]
    s = sin[None, :, None, None, :]
    x1 = x[..., :half]
    x2 = x[..., half:ROPE_DIMS]
    return jnp.concatenate([x1 * c - x2 * s, x2 * c + x1 * s, x[..., ROPE_DIMS:]], axis=-1)


def causal_conv(buf, u, w):
    L = u.shape[1]
    xc = jnp.concatenate([buf.astype(u.dtype), u], axis=1)
    y = xc[:, 0:L] * w[0]
    for j in range(1, CONV_WIDTH):
        y = y + xc[:, j:j + L] * w[j]
    return y, xc[:, L:]


def _to_chunks(t, n, c):
    b = t.shape[0]
    pad = n * c - t.shape[1]
    t = jnp.pad(t, [(0, 0), (0, pad)] + [(0, 0)] * (t.ndim - 2))
    t = t.reshape((b, n, c) + t.shape[2:])
    return jnp.transpose(t, (1, 0, 3, 2) + tuple(range(4, t.ndim)))


def gated_delta_rule(q, k, v, g, beta, s0):
    b, L, h, _ = q.shape
    c = min(GDN_CHUNK, L)
    n = -(-L // c)
    qc = _to_chunks(q, n, c)
    kc = _to_chunks(k, n, c)
    vc = _to_chunks(v, n, c)
    gc = jnp.cumsum(_to_chunks(g, n, c), axis=-1)
    bc = _to_chunks(beta, n, c)
    lower = jnp.tril(jnp.ones((c, c), dtype=bool))
    decay = jnp.exp(jnp.where(lower, gc[..., :, None] - gc[..., None, :], -jnp.inf))
    kb = kc * bc[..., None]
    a = jnp.tril(jnp.einsum('nbhik,nbhjk->nbhij', kb, kc) * decay, -1)
    eye = jnp.eye(c, dtype=F32)
    t_inv = lax.linalg.triangular_solve(eye + a, jnp.broadcast_to(eye, a.shape), left_side=True, lower=True)
    u = t_inv @ (vc * bc[..., None])
    w = t_inv @ (kb * jnp.exp(gc)[..., None])
    qk = jnp.einsum('nbhik,nbhjk->nbhij', qc, kc) * decay

    def step(s, inp):
        qi, ki, ui, wi, gi, qki = inp
        v_new = ui - jnp.einsum('bhck,bhkv->bhcv', wi, s)
        o = (jnp.einsum('bhck,bhkv->bhcv', qi * jnp.exp(gi)[..., None], s)
             + jnp.einsum('bhij,bhjv->bhiv', qki, v_new))
        g_last = gi[..., -1:]
        s = (s * jnp.exp(g_last)[..., None]
             + jnp.einsum('bhck,bhcv->bhkv', ki * jnp.exp(g_last - gi)[..., None], v_new))
        return s, o

    s, o = lax.scan(step, s0, (qc, kc, u, w, gc, qk))
    o = jnp.transpose(o, (1, 0, 3, 2, 4)).reshape(b, n * c, h, -1)[:, :L]
    return o, s


def gdn_mixer(h, conv_buf, s0, w_in, conv_w, a_log, dt_bias, out_gain, w_out):
    b, L, _ = h.shape
    proj = h @ w_in
    qkv, z, beta_in, a_in = jnp.split(
        proj, [GDN_CONV_DIM, GDN_CONV_DIM + GDN_V_WIDTH, GDN_CONV_DIM + GDN_V_WIDTH + GDN_V_HEADS], axis=-1)
    qkv, new_buf = causal_conv(conv_buf, qkv, conv_w)
    qkv = jax.nn.silu(qkv.astype(F32))
    q, k, v = jnp.split(qkv, [GDN_QK_WIDTH, 2 * GDN_QK_WIDTH], axis=-1)
    rep = GDN_V_HEADS // GDN_QK_HEADS
    q = jnp.repeat(l2_normalize(q.reshape(b, L, GDN_QK_HEADS, GDN_DK)) * GDN_DK ** -0.5, rep, axis=2)
    k = jnp.repeat(l2_normalize(k.reshape(b, L, GDN_QK_HEADS, GDN_DK)), rep, axis=2)
    v = v.reshape(b, L, GDN_V_HEADS, GDN_DV)
    beta = jax.nn.sigmoid(beta_in.astype(F32))
    g = -jnp.exp(a_log.astype(F32)) * jax.nn.softplus(a_in.astype(F32) + dt_bias.astype(F32))
    o, s = gated_delta_rule(q, k, v, g, beta, s0.astype(F32))
    o = rms_norm(o, out_gain) * jax.nn.silu(z.astype(F32)).reshape(b, L, GDN_V_HEADS, GDN_DV)
    y = o.reshape(b, L, GDN_V_WIDTH).astype(h.dtype) @ w_out
    return y, new_buf, s.astype(s0.dtype)


def shared_kv(x, c, pos, ada_w_kv, ada_b_kv, norm_kv, w_kv, k_gain):
    b, L, _ = x.shape
    shift, scale = ada_split(c, ada_w_kv, ada_b_kv, 2)
    h = modulate(x, norm_kv, shift, scale)
    k, v = jnp.split(h @ w_kv, [DIFF_WIDTH], axis=-1)
    cos, sin = rotary_tables(pos)
    k = partial_rotary(rms_norm(k.reshape(b, L, DIFF_HEADS, 2, DIFF_DH).astype(F32), k_gain), cos, sin)
    k = k.reshape(b, L, DIFF_HEADS, 2 * DIFF_DH).astype(x.dtype)
    v = v.reshape(b, L, DIFF_HEADS, DIFF_VH)
    return k, v


def diff_block(q, k, v, q_pos, k_pos, lam):
    s = jnp.einsum('bqhmd,bkhmd->bhmqk', q, k) * DIFF_DH ** -0.5
    s = jnp.where((k_pos[None, :] <= q_pos[:, None])[None, None, None], s, -jnp.inf)
    p = jax.nn.softmax(s, axis=-1)
    p = p[:, :, 0] - lam * p[:, :, 1]
    return jnp.einsum('bhqk,bkhe->bqhe', p, v)


def diff_mixer(h, k_all, v_all, q_pos, k_pos, layer_idx, w_in, q_gain, lam_p, sub_gain, w_out):
    b, L, _ = h.shape
    q, z = jnp.split(h @ w_in, [DIFF_WIDTH], axis=-1)
    cos, sin = rotary_tables(q_pos)
    q = partial_rotary(rms_norm(q.reshape(b, L, DIFF_HEADS, 2, DIFF_DH).astype(F32), q_gain), cos, sin)
    lam_init = 0.8 - 0.6 * math.exp(-0.3 * layer_idx)
    lp = lam_p.astype(F32)
    lam = jnp.exp(jnp.sum(lp[0] * lp[1])) - jnp.exp(jnp.sum(lp[2] * lp[3])) + lam_init
    kf = k_all.reshape(b, k_all.shape[1], DIFF_HEADS, 2, DIFF_DH).astype(F32)
    vf = v_all.astype(F32)
    qb = Q_BLOCK if L % Q_BLOCK == 0 else L
    nb = L // qb
    qs = jnp.swapaxes(q.reshape(b, nb, qb, DIFF_HEADS, 2, DIFF_DH), 0, 1)
    ps = q_pos.reshape(nb, qb)
    o = lax.map(lambda t: diff_block(t[0], kf, vf, t[1], k_pos, lam), (qs, ps))
    o = jnp.swapaxes(o, 0, 1).reshape(b, L, DIFF_HEADS, DIFF_VH)
    o = rms_norm(o, sub_gain) * (1.0 - lam_init)
    o = o * jax.nn.silu(z.astype(F32)).reshape(b, L, DIFF_HEADS, DIFF_VH)
    return o.reshape(b, L, DIFF_WIDTH).astype(h.dtype) @ w_out


def run_group(x, c, past_len, conv_bufs, gdn_states, k_past, v_past, p):
    L = x.shape[1]
    pos = past_len + jnp.arange(L, dtype=jnp.int32)
    k_pos = jnp.arange(past_len + L, dtype=jnp.int32)
    new_bufs, new_states = [], []
    k_new = v_new = k_all = v_all = None
    for l in range(DEPTH):
        if l < N_A_LAYERS:
            shift, scale, gate = ada_split(c, p['ada_w_a'][l], p['ada_b_a'][l], 3)
            h = modulate(x, p['norm_a'][l], shift, scale)
            y, buf, s = gdn_mixer(h, conv_bufs[l], gdn_states[l], p['w_in_a'][l], p['conv_w_a'][l],
                                  p['a_log'][l], p['dt_bias'][l], p['gdn_out_gain'][l], p['w_out_a'][l])
            x = x + gate[:, None, :] * y
            new_bufs.append(buf)
            new_states.append(s)
            if l == N_A_LAYERS - 1:
                k_new, v_new = shared_kv(x, c, pos, p['ada_w_kv'], p['ada_b_kv'], p['norm_kv'],
                                         p['w_kv'], p['k_gain'])
                k_all = jnp.concatenate([k_past.astype(k_new.dtype), k_new], axis=1)
                v_all = jnp.concatenate([v_past.astype(v_new.dtype), v_new], axis=1)
        else:
            j = l - N_A_LAYERS
            shift, scale, gate = ada_split(c, p['ada_w_b'][j], p['ada_b_b'][j], 3)
            h = modulate(x, p['norm_b'][j], shift, scale)
            y = diff_mixer(h, k_all, v_all, pos, k_pos, l, p['w_in_b'][j], p['q_gain'][j],
                           p['lam_params'][j], p['subln_gain'][j], p['w_out_b'][j])
            x = x + gate[:, None, :] * y
    return x, jnp.stack(new_bufs), jnp.stack(new_states), k_new, v_new


def setup_inputs(seed: int = 0) -> dict:
    key = jax.random.key(seed)
    ks = iter(jax.random.split(key, 48))

    def nrm(shape, scale):
        return jax.random.normal(next(ks), shape, F32) * scale

    def gain(shape):
        return 1.0 + nrm(shape, 0.01)

    n_pages = PAST_LEN // PAGE_SIZE
    n_phys = (DEC_BATCH * n_pages * 5) // 4
    D = D_MODEL
    ada_s = 0.5 * D ** -0.5
    x_prompt = nrm((BATCH, SEQ, D), 1.0)
    x_sample = nrm((DEC_BATCH, DEC_SEQ, D), 1.0)
    state_gdn = nrm((N_A_LAYERS, DEC_BATCH, GDN_V_HEADS, GDN_DK, GDN_DV), GDN_DK ** -0.5)
    state_conv = nrm((N_A_LAYERS, DEC_BATCH, CONV_WIDTH - 1, GDN_CONV_DIM), 1.0)
    cache_k = nrm((n_phys, PAGE_SIZE, DIFF_HEADS, 2 * DIFF_DH), 1.0)
    cache_v = nrm((n_phys, PAGE_SIZE, DIFF_HEADS, DIFF_VH), 1.0)
    page_table = jax.random.permutation(next(ks), n_phys)[:DEC_BATCH * n_pages].reshape(
        DEC_BATCH, n_pages).astype(jnp.int32)
    c_prompt = nrm((BATCH, D), 1.0)
    c_sample = nrm((DEC_BATCH, D), 1.0)
    ada_w_a = nrm((N_A_LAYERS, D, 3 * D), ada_s)
    ada_b_a = nrm((N_A_LAYERS, 3 * D), 0.01)
    norm_a = gain((N_A_LAYERS, D))
    w_in_a = nrm((N_A_LAYERS, D, GDN_IN_WIDTH), D ** -0.5)
    conv_w_a = nrm((N_A_LAYERS, CONV_WIDTH, GDN_CONV_DIM), CONV_WIDTH ** -0.5)
    a_log = jnp.log(jax.random.uniform(next(ks), (N_A_LAYERS, GDN_V_HEADS), F32, 1.0, 16.0))
    dt = jnp.exp(jax.random.uniform(next(ks), (N_A_LAYERS, GDN_V_HEADS), F32,
                                    math.log(1e-3), math.log(1e-1)))
    dt_bias = dt + jnp.log(-jnp.expm1(-dt))
    gdn_out_gain = gain((N_A_LAYERS, GDN_DV))
    w_out_a = nrm((N_A_LAYERS, GDN_V_WIDTH, D), GDN_V_WIDTH ** -0.5)
    ada_w_kv = nrm((D, 2 * D), ada_s)
    ada_b_kv = nrm((2 * D,), 0.01)
    norm_kv = gain((D,))
    w_kv = nrm((D, 2 * DIFF_WIDTH), D ** -0.5)
    k_gain = gain((DIFF_DH,))
    ada_w_b = nrm((N_B_LAYERS, D, 3 * D), ada_s)
    ada_b_b = nrm((N_B_LAYERS, 3 * D), 0.01)
    norm_b = gain((N_B_LAYERS, D))
    w_in_b = nrm((N_B_LAYERS, D, 2 * DIFF_WIDTH), D ** -0.5)
    q_gain = gain((N_B_LAYERS, DIFF_DH))
    lam_params = nrm((N_B_LAYERS, 4, DIFF_DH), 0.1)
    subln_gain = gain((N_B_LAYERS, DIFF_VH))
    w_out_b = nrm((N_B_LAYERS, DIFF_WIDTH, D), DIFF_WIDTH ** -0.5)
    return {'x_prompt': x_prompt, 'x_sample': x_sample, 'state_gdn': state_gdn, 'state_conv': state_conv,
            'cache_k': cache_k, 'cache_v': cache_v, 'page_table': page_table,
            'c_prompt': c_prompt, 'c_sample': c_sample,
            'ada_w_a': ada_w_a, 'ada_b_a': ada_b_a, 'norm_a': norm_a, 'w_in_a': w_in_a, 'conv_w_a': conv_w_a,
            'a_log': a_log, 'dt_bias': dt_bias, 'gdn_out_gain': gdn_out_gain, 'w_out_a': w_out_a,
            'ada_w_kv': ada_w_kv, 'ada_b_kv': ada_b_kv, 'norm_kv': norm_kv, 'w_kv': w_kv, 'k_gain': k_gain,
            'ada_w_b': ada_w_b, 'ada_b_b': ada_b_b, 'norm_b': norm_b, 'w_in_b': w_in_b, 'q_gain': q_gain,
            'lam_params': lam_params, 'subln_gain': subln_gain, 'w_out_b': w_out_b}


def reference(x_prompt, x_sample, state_gdn, state_conv, cache_k, cache_v, page_table, c_prompt, c_sample,
              ada_w_a, ada_b_a, norm_a, w_in_a, conv_w_a, a_log, dt_bias, gdn_out_gain, w_out_a,
              ada_w_kv, ada_b_kv, norm_kv, w_kv, k_gain,
              ada_w_b, ada_b_b, norm_b, w_in_b, q_gain, lam_params, subln_gain, w_out_b):
    p = dict(ada_w_a=ada_w_a, ada_b_a=ada_b_a, norm_a=norm_a, w_in_a=w_in_a, conv_w_a=conv_w_a,
             a_log=a_log, dt_bias=dt_bias, gdn_out_gain=gdn_out_gain, w_out_a=w_out_a,
             ada_w_kv=ada_w_kv, ada_b_kv=ada_b_kv, norm_kv=norm_kv, w_kv=w_kv, k_gain=k_gain,
             ada_w_b=ada_w_b, ada_b_b=ada_b_b, norm_b=norm_b, w_in_b=w_in_b, q_gain=q_gain,
             lam_params=lam_params, subln_gain=subln_gain, w_out_b=w_out_b)
    b = x_prompt.shape[0]
    conv0 = jnp.zeros((N_A_LAYERS, b, CONV_WIDTH - 1, GDN_CONV_DIM), x_prompt.dtype)
    st0 = jnp.zeros((N_A_LAYERS, b, GDN_V_HEADS, GDN_DK, GDN_DV), state_gdn.dtype)
    k0 = jnp.zeros((b, 0, DIFF_HEADS, 2 * DIFF_DH), x_prompt.dtype)
    v0 = jnp.zeros((b, 0, DIFF_HEADS, DIFF_VH), x_prompt.dtype)
    y_prompt, conv_p, st_p, k_p, v_p = run_group(x_prompt, c_prompt, 0, conv0, st0, k0, v0, p)
    db = x_sample.shape[0]
    past = page_table.shape[1] * PAGE_SIZE
    k_past = cache_k[page_table].reshape(db, past, DIFF_HEADS, 2 * DIFF_DH)
    v_past = cache_v[page_table].reshape(db, past, DIFF_HEADS, DIFF_VH)
    y_sample, conv_s, st_s, k_s, v_s = run_group(x_sample, c_sample, past, state_conv, state_gdn,
                                                 k_past, v_past, p)
    return (y_prompt, y_sample, st_p, conv_p, k_p, v_p, st_s, conv_s, k_s, v_s)
```

```python
import functools
import math

import jax
import jax.numpy as jnp
from jax import lax
from jax.experimental import pallas as pl
from jax.experimental.pallas import tpu as pltpu

F32 = jnp.float32
BF16 = jnp.bfloat16
NORM_EPS = 1e-6

GDN_DK = 128
GDN_DV = 128
CONV_WIDTH = 4
GDN_CHUNK = 64
TRI_LEAF = 8
DIFF_DH = 64
DIFF_VH = 2 * DIFF_DH
ROPE_DIMS = DIFF_DH // 4
ROPE_THETA = 500000.0
PAGE_SIZE = 128
MASK_VALUE = -1e30

V7X_VMEM_BYTES = 64 * 1024 * 1024
SUBLANES = 8
LANES = 128

_NT = (((1,), (1,)), ((), ()))
_TN = (((0,), (0,)), ((), ()))


def _vmem_limit(nbytes):
    return int(min(nbytes + (8 << 20), V7X_VMEM_BYTES - (6 << 20)))


def _sigmoid(x):
    return 1.0 / (1.0 + jnp.exp(-x))


def _silu(x):
    return x * _sigmoid(x)


def _dot(a, b):
    return jnp.dot(a.astype(BF16), b.astype(BF16), preferred_element_type=F32)


def _dot_nt(a, b):
    return lax.dot_general(a.astype(BF16), b.astype(BF16), _NT, preferred_element_type=F32)


def _dot_tn(a, b):
    return lax.dot_general(a.astype(BF16), b.astype(BF16), _TN, preferred_element_type=F32)


def _rows_spec(tm, width, per_row):
    if per_row:
        return pl.BlockSpec((None, tm, width), lambda g, i: (g, i, 0))
    return pl.BlockSpec((None, 1, width), lambda g, i: (g, 0, 0))


def _const_spec(shape):
    return pl.BlockSpec(shape, lambda g, i: (0,) * len(shape), pipeline_mode=pl.Buffered(1))


def _ada_kernel(c_ref, w_ref, b_ref, o_ref):
    o_ref[...] = jnp.dot(c_ref[...], w_ref[...], preferred_element_type=F32) + b_ref[...]


def _ada(c, w, b):
    rows, d = c.shape
    n = w.shape[1]
    tn = 1024
    return pl.pallas_call(
        _ada_kernel,
        grid=(n // tn,),
        in_specs=[pl.BlockSpec((rows, d), lambda j: (0, 0)),
                  pl.BlockSpec((d, tn), lambda j: (0, j)),
                  pl.BlockSpec((1, tn), lambda j: (0, j))],
        out_specs=pl.BlockSpec((rows, tn), lambda j: (0, j)),
        out_shape=jax.ShapeDtypeStruct((rows, n), F32),
        name="ada",
    )(c, w, b)


def _gdn_in_kernel(x_ref, shift_ref, scale_ref, gain_ref, w_ref, cw_ref, alog_ref, dtb_ref, conv0_ref,
                   q_ref, k_ref, v_ref, z_ref, g_ref, beta_ref, convo_ref, xc_ref,
                   *, tm, stride, nt, qk_width, v_width, n_vheads):
    i = pl.program_id(1)
    conv_dim = 2 * qk_width + v_width
    hist = (CONV_WIDTH - 1) * stride
    hist_base = -(-hist // SUBLANES) * SUBLANES

    x = x_ref[...]
    h = x * lax.rsqrt(jnp.mean(x * x, axis=-1, keepdims=True) + NORM_EPS) * gain_ref[...]
    h = h * (1.0 + scale_ref[...]) + shift_ref[...]
    hb = h.astype(BF16)

    @pl.when(i == 0)
    def _():
        xc_ref[hist_base - hist:hist_base, :] = conv0_ref[...]

    n_qk_heads = qk_width // GDN_DK
    for c0 in range(0, conv_dim, qk_width):
        cols = slice(c0, c0 + qk_width)
        xc_ref[hist_base:hist_base + tm, cols] = jnp.dot(hb, w_ref[:, cols], preferred_element_type=F32)
        y = xc_ref[hist_base - hist:hist_base - hist + tm, cols] * cw_ref[0:1, cols]
        for j in range(1, CONV_WIDTH):
            r0 = hist_base - hist + j * stride
            y = y + xc_ref[r0:r0 + tm, cols] * cw_ref[j:j + 1, cols]
        a = _silu(y)
        if c0 < 2 * qk_width:
            out_ref = q_ref if c0 == 0 else k_ref
            post = GDN_DK ** -0.5 if c0 == 0 else 1.0
            for hd in range(n_qk_heads):
                ah = a[:, hd * GDN_DK:(hd + 1) * GDN_DK]
                nrm = lax.rsqrt(jnp.sum(ah * ah, axis=-1, keepdims=True) + NORM_EPS)
                out_ref[:, hd * GDN_DK:(hd + 1) * GDN_DK] = ah * nrm * post
        else:
            v0 = c0 - 2 * qk_width
            v_ref[:, v0:v0 + qk_width] = a

    new_hist = xc_ref[hist_base + tm - hist:hist_base + tm, :]
    xc_ref[hist_base - hist:hist_base, :] = new_hist

    @pl.when(i == nt - 1)
    def _():
        convo_ref[...] = new_hist

    for c0 in range(0, v_width, qk_width):
        zc = jnp.dot(hb, w_ref[:, conv_dim + c0:conv_dim + c0 + qk_width], preferred_element_type=F32)
        z_ref[:, c0:c0 + qk_width] = _silu(zc)

    g0 = conv_dim + v_width
    bg = jnp.dot(hb, w_ref[:, g0:g0 + 2 * n_vheads], preferred_element_type=F32)
    beta_ref[...] = _sigmoid(bg[:, :n_vheads])
    a_in = bg[:, n_vheads:] + dtb_ref[...]
    softplus = jnp.maximum(a_in, 0.0) + jnp.log1p(jnp.exp(-jnp.abs(a_in)))
    g_ref[...] = -jnp.exp(alog_ref[...]) * softplus


def _gdn_in(x, shift, scale, gain, w_in, conv_w, a_log, dt_bias, conv0, *, tm, stride):
    g_, t_, d = x.shape
    n_vheads = a_log.shape[-1]
    conv_dim = conv_w.shape[-1]
    v_width = n_vheads * GDN_DV
    qk_width = (conv_dim - v_width) // 2
    in_width = w_in.shape[1]
    nt = t_ // tm
    hist = (CONV_WIDTH - 1) * stride
    hist_base = -(-hist // SUBLANES) * SUBLANES
    per_row = shift.shape[1] != 1
    kern = functools.partial(_gdn_in_kernel, tm=tm, stride=stride, nt=nt, qk_width=qk_width,
                             v_width=v_width, n_vheads=n_vheads)
    est = (w_in.size * 2 + 2 * tm * d * 4 + 2 * tm * (2 * qk_width + 2 * v_width) * 4
           + (hist_base + tm) * conv_dim * 4 + 4 * tm * qk_width * 4 + (4 * tm * d * 4 if per_row else 0))
    outs = pl.pallas_call(
        kern,
        grid=(g_, nt),
        in_specs=[_rows_spec(tm, d, True), _rows_spec(tm, d, per_row), _rows_spec(tm, d, per_row),
                  _const_spec((1, d)), _const_spec((d, in_width)), _const_spec((CONV_WIDTH, conv_dim)),
                  _const_spec((1, n_vheads)), _const_spec((1, n_vheads)),
                  pl.BlockSpec((None, hist, conv_dim), lambda g, i: (g, 0, 0))],
        out_specs=[_rows_spec(tm, qk_width, True), _rows_spec(tm, qk_width, True),
                   _rows_spec(tm, v_width, True), _rows_spec(tm, v_width, True),
                   _rows_spec(tm, n_vheads, True), _rows_spec(tm, n_vheads, True),
                   pl.BlockSpec((None, hist, conv_dim), lambda g, i: (g, 0, 0))],
        out_shape=[jax.ShapeDtypeStruct((g_, t_, qk_width), F32), jax.ShapeDtypeStruct((g_, t_, qk_width), F32),
                   jax.ShapeDtypeStruct((g_, t_, v_width), F32), jax.ShapeDtypeStruct((g_, t_, v_width), F32),
                   jax.ShapeDtypeStruct((g_, t_, n_vheads), F32), jax.ShapeDtypeStruct((g_, t_, n_vheads), F32),
                   jax.ShapeDtypeStruct((g_, hist, conv_dim), F32)],
        scratch_shapes=[pltpu.VMEM((hist_base + tm, conv_dim), F32)],
        compiler_params=pltpu.CompilerParams(dimension_semantics=("arbitrary", "arbitrary"),
                                             vmem_limit_bytes=_vmem_limit(est)),
        name="gdn_in",
    )(x, shift, scale, gain, w_in, conv_w, a_log, dt_bias, conv0)
    return outs


def _gdn_kernel(q_ref, k_ref, v_ref, z_ref, g_ref, gt_ref, beta_ref, s0_ref, gain_ref,
                og_ref, sout_ref, s_ref, *, c, nc, n_qk_heads, rep):
    n = pl.program_id(1)

    @pl.when(n == 0)
    def _():
        s_ref[...] = s0_ref[...]

    row = lax.broadcasted_iota(jnp.int32, (c, c), 0)
    col = lax.broadcasted_iota(jnp.int32, (c, c), 1)
    lower = row >= col
    strict = row > col
    gc_all = jnp.dot(lower.astype(F32), g_ref[...], precision=lax.Precision.HIGHEST,
                     preferred_element_type=F32)
    gcr_all = jnp.dot(gt_ref[...], (row <= col).astype(F32), precision=lax.Precision.HIGHEST,
                      preferred_element_type=F32)
    beta_all = beta_ref[...]
    eye = (row == col).astype(F32)
    leaf = min(c, TRI_LEAF)
    same_block = lambda m: (row >> int(math.log2(m))) == (col >> int(math.log2(m)))
    leaf_mask = same_block(leaf)
    merge_masks = []
    m = leaf
    while m < c:
        merge_masks.append(same_block(2 * m) & jnp.logical_not(same_block(m)))
        m *= 2

    def unit_lower_inverse(a):
        d = jnp.where(leaf_mask, a, 0.0)
        t = eye - d
        p = d
        for _ in range(int(math.log2(leaf)) - 1):
            p = _dot(p, p)
            t = t + _dot(t, p)
        for mask in merge_masks:
            t = t - _dot(t, _dot(jnp.where(mask, a, 0.0), t))
        return t

    for j in range(n_qk_heads):
        qj = q_ref[:, j * GDN_DK:(j + 1) * GDN_DK]
        kj = k_ref[:, j * GDN_DK:(j + 1) * GDN_DK]
        kk = _dot_nt(kj, kj)
        qk = _dot_nt(qj, kj)
        for r in range(rep):
            hv = j * rep + r
            gc = gc_all[:, hv:hv + 1]
            gcr = gcr_all[hv:hv + 1, :]
            beta = beta_all[:, hv:hv + 1]
            dec = jnp.exp(jnp.where(lower, gc - gcr, MASK_VALUE))
            a = jnp.where(strict, kk * beta * dec, 0.0)
            eg = jnp.exp(gc)
            vh = v_ref[:, hv * GDN_DV:(hv + 1) * GDN_DV]
            rhs = jnp.concatenate([vh * beta, kj * (beta * eg)], axis=1)
            xs = _dot(unit_lower_inverse(a), rhs)
            u = xs[:, :GDN_DV]
            w = xs[:, GDN_DV:]
            s = s_ref[hv]
            v_new = u - _dot(w, s)
            o = _dot(qj * eg, s) + _dot(qk * dec, v_new)
            g_last = gc[c - 1:c, :]
            s_ref[hv] = s * jnp.exp(g_last) + _dot_tn(kj * jnp.exp(g_last - gc), v_new)
            on = o * lax.rsqrt(jnp.mean(o * o, axis=-1, keepdims=True) + NORM_EPS) * gain_ref[...]
            og_ref[:, hv * GDN_DV:(hv + 1) * GDN_DV] = (on * z_ref[:, hv * GDN_DV:(hv + 1) * GDN_DV]).astype(BF16)

    @pl.when(n == nc - 1)
    def _():
        sout_ref[...] = s_ref[...]


def _gdn(q, k, v, z, g, beta, s0, out_gain):
    b_, l_, qk_width = q.shape
    v_width = v.shape[-1]
    hv = g.shape[-1]
    n_qk_heads = qk_width // GDN_DK
    rep = hv // n_qk_heads
    c = min(GDN_CHUNK, l_)
    nc = l_ // c
    gt = jnp.swapaxes(g.reshape(b_, nc, c, hv), 2, 3)
    kern = functools.partial(_gdn_kernel, c=c, nc=nc, n_qk_heads=n_qk_heads, rep=rep)
    row = lambda w: pl.BlockSpec((None, c, w), lambda b, n: (b, n, 0))
    state = pl.BlockSpec((None, hv, GDN_DK, GDN_DV), lambda b, n: (b, 0, 0, 0))
    og, s_out = pl.pallas_call(
        kern,
        grid=(b_, nc),
        in_specs=[row(qk_width), row(qk_width), row(v_width), row(v_width), row(hv),
                  pl.BlockSpec((None, None, hv, c), lambda b, n: (b, n, 0, 0)), row(hv), state,
                  pl.BlockSpec((1, GDN_DV), lambda b, n: (0, 0))],
        out_specs=[row(v_width), state],
        out_shape=[jax.ShapeDtypeStruct((b_, l_, v_width), BF16),
                   jax.ShapeDtypeStruct((b_, hv, GDN_DK, GDN_DV), F32)],
        scratch_shapes=[pltpu.VMEM((hv, GDN_DK, GDN_DV), F32)],
        compiler_params=pltpu.CompilerParams(dimension_semantics=("arbitrary", "arbitrary")),
        name="gdn",
    )(q, k, v, z, g, gt, beta, s0, out_gain)
    return og, s_out


def _comp_norm_rotary(xh, gain, cosf, sin_lo, sin_hi):
    lane = lax.broadcasted_iota(jnp.int32, xh.shape, 1)
    first = lane < DIFF_DH
    sq = xh * xh
    s_a = jnp.sum(jnp.where(first, sq, 0.0), axis=-1, keepdims=True)
    s_b = jnp.sum(jnp.where(first, 0.0, sq), axis=-1, keepdims=True)
    ms = jnp.where(first, s_a, s_b) * (1.0 / DIFF_DH)
    y = xh * lax.rsqrt(ms + NORM_EPS) * gain
    half = ROPE_DIMS // 2
    return y * cosf + pltpu.roll(y, LANES - half, 1) * sin_lo + pltpu.roll(y, half, 1) * sin_hi


def _mid_kernel(og_ref, x_ref, gate_ref, wo_ref,
                shkv_ref, sckv_ref, nkv_ref, wkv_ref, kg_ref,
                shb_ref, scb_ref, nb_ref, wb_ref, qg_ref,
                cos_ref, slo_ref, shi_ref,
                x1_ref, k_ref, v_ref, kb_ref, vb_ref, q_ref, z_ref, *, n_heads):
    x1 = x_ref[...] + gate_ref[...] * jnp.dot(og_ref[...], wo_ref[...], preferred_element_type=F32)
    x1_ref[...] = x1
    r = x1 * lax.rsqrt(jnp.mean(x1 * x1, axis=-1, keepdims=True) + NORM_EPS)
    cosf, slo, shi = cos_ref[...], slo_ref[...], shi_ref[...]
    width = n_heads * DIFF_VH

    hkv = ((r * nkv_ref[...]) * (1.0 + sckv_ref[...]) + shkv_ref[...]).astype(BF16)
    kraw = jnp.dot(hkv, wkv_ref[:, :width], preferred_element_type=F32)
    for hd in range(n_heads):
        sl = slice(hd * DIFF_VH, (hd + 1) * DIFF_VH)
        kh = _comp_norm_rotary(kraw[:, sl], kg_ref[...], cosf, slo, shi)
        k_ref[:, sl] = kh
        kb_ref[:, sl] = kh.astype(BF16)
    vraw = jnp.dot(hkv, wkv_ref[:, width:], preferred_element_type=F32)
    v_ref[...] = vraw
    vb_ref[...] = vraw.astype(BF16)

    hq = ((r * nb_ref[...]) * (1.0 + scb_ref[...]) + shb_ref[...]).astype(BF16)
    qraw = jnp.dot(hq, wb_ref[:, :width], preferred_element_type=F32)
    for hd in range(n_heads):
        sl = slice(hd * DIFF_VH, (hd + 1) * DIFF_VH)
        qh = _comp_norm_rotary(qraw[:, sl], qg_ref[...], cosf, slo, shi)
        q_ref[:, sl] = (qh * DIFF_DH ** -0.5).astype(BF16)
    z_ref[...] = _silu(jnp.dot(hq, wb_ref[:, width:], preferred_element_type=F32))


def _mid(og, x, gate, w_out, shift_kv, scale_kv, norm_kv, w_kv, k_gain,
         shift_b, scale_b, norm_b, w_b, q_gain, cosf, sin_lo, sin_hi, *, tm):
    g_, t_, d = x.shape
    v_width = og.shape[-1]
    width = w_kv.shape[1] // 2
    n_heads = width // DIFF_VH
    per_row = gate.shape[1] != 1
    rope_per_group = cosf.shape[0] != 1
    rope_spec = pl.BlockSpec((None, tm, LANES), (lambda g, i: (g, i, 0)) if rope_per_group else (lambda g, i: (0, i, 0)))
    mod = _rows_spec(tm, d, per_row)
    est = ((w_out.size + w_kv.size + w_b.size) * 2 + 2 * tm * (v_width * 2 + d * 4)
           + 2 * tm * width * (4 + 4 + 4 + 2 + 2 + 2 + 4) + 8 * tm * width * 4 + (10 * tm * d * 4 if per_row else 0))
    kern = functools.partial(_mid_kernel, n_heads=n_heads)
    f32_out = jax.ShapeDtypeStruct((g_, t_, width), F32)
    bf_out = jax.ShapeDtypeStruct((g_, t_, width), BF16)
    return pl.pallas_call(
        kern,
        grid=(g_, t_ // tm),
        in_specs=[_rows_spec(tm, v_width, True), _rows_spec(tm, d, True), mod, _const_spec((v_width, d)),
                  mod, mod, _const_spec((1, d)), _const_spec((d, 2 * width)), _const_spec((1, DIFF_VH)),
                  mod, mod, _const_spec((1, d)), _const_spec((d, 2 * width)), _const_spec((1, DIFF_VH)),
                  rope_spec, rope_spec, rope_spec],
        out_specs=[_rows_spec(tm, d, True)] + [_rows_spec(tm, width, True)] * 6,
        out_shape=[jax.ShapeDtypeStruct((g_, t_, d), F32), f32_out, f32_out, bf_out, bf_out, bf_out, f32_out],
        compiler_params=pltpu.CompilerParams(dimension_semantics=("arbitrary", "arbitrary"),
                                             vmem_limit_bytes=_vmem_limit(est)),
        name="mid",
    )(og, x, gate, w_out, shift_kv, scale_kv, norm_kv, w_kv, k_gain,
      shift_b, scale_b, norm_b, w_b, q_gain, cosf, sin_lo, sin_hi)


def _lambda(lam_ref, lam_init):
    lp = lam_ref[...]
    a = jnp.sum(lp[0:1] * lp[1:2], axis=-1, keepdims=True)
    b = jnp.sum(lp[2:3] * lp[3:4], axis=-1, keepdims=True)
    return jnp.exp(a) - jnp.exp(b) + lam_init


def _subln_gate(o, gain, z, lam_init):
    on = o * lax.rsqrt(jnp.mean(o * o, axis=-1, keepdims=True) + NORM_EPS) * gain
    return (on * (1.0 - lam_init)) * z


def _prompt_attn_kernel(q_ref, k_ref, v_ref, z_ref, lam_ref, gain_ref, o_ref, acc1_ref, acc2_ref,
                        *, t, lam_init):
    i = pl.program_id(2)
    q = q_ref[...]
    lane = lax.broadcasted_iota(jnp.int32, q.shape, 1)
    zero = jnp.zeros_like(q)
    q1 = jnp.where(lane < DIFF_DH, q, zero)
    q2 = jnp.where(lane < DIFF_DH, zero, q)
    acc1_ref[...] = jnp.zeros_like(acc1_ref)
    acc2_ref[...] = jnp.zeros_like(acc2_ref)

    def update(s, m, l, acc_ref, vj):
        m_new = jnp.maximum(m, jnp.max(s, axis=-1, keepdims=True))
        p = jnp.exp(s - m_new)
        alpha = jnp.exp(m - m_new)
        acc_ref[...] = alpha * acc_ref[...] + jnp.dot(p.astype(BF16), vj, preferred_element_type=F32)
        return m_new, alpha * l + jnp.sum(p, axis=-1, keepdims=True)

    def step(j, carry, masked):
        m1, l1, m2, l2 = carry
        start = pl.multiple_of(j * t, t)
        kj = k_ref[pl.ds(start, t), :]
        vj = v_ref[pl.ds(start, t), :]
        s1 = lax.dot_general(q1, kj, _NT, preferred_element_type=F32)
        s2 = lax.dot_general(q2, kj, _NT, preferred_element_type=F32)
        if masked:
            keep = lax.broadcasted_iota(jnp.int32, (t, t), 0) >= lax.broadcasted_iota(jnp.int32, (t, t), 1)
            s1 = jnp.where(keep, s1, MASK_VALUE)
            s2 = jnp.where(keep, s2, MASK_VALUE)
        m1, l1 = update(s1, m1, l1, acc1_ref, vj)
        m2, l2 = update(s2, m2, l2, acc2_ref, vj)
        return m1, l1, m2, l2

    neg = jnp.full((t, 1), MASK_VALUE, F32)
    zer = jnp.zeros((t, 1), F32)
    carry = lax.fori_loop(0, i, lambda j, cr: step(j, cr, False), (neg, zer, neg, zer))
    m1, l1, m2, l2 = step(i, carry, True)
    lam = _lambda(lam_ref, lam_init)
    o = acc1_ref[...] / l1 - lam * (acc2_ref[...] / l2)
    o_ref[...] = _subln_gate(o, gain_ref[...], z_ref[...], lam_init).astype(BF16)


def _prompt_attn(q, k, v, z, lam_params, sub_gain, *, t, lam_init):
    b_, l_, width = q.shape
    n_heads = width // DIFF_VH
    kern = functools.partial(_prompt_attn_kernel, t=t, lam_init=lam_init)
    tile = pl.BlockSpec((None, t, DIFF_VH), lambda b, h, i: (b, i, h))
    full = pl.BlockSpec((None, l_, DIFF_VH), lambda b, h, i: (b, 0, h))
    return pl.pallas_call(
        kern,
        grid=(b_, n_heads, l_ // t),
        in_specs=[tile, full, full, tile,
                  pl.BlockSpec(lam_params.shape, lambda b, h, i: (0, 0)),
                  pl.BlockSpec((1, DIFF_VH), lambda b, h, i: (0, 0))],
        out_specs=tile,
        out_shape=jax.ShapeDtypeStruct((b_, l_, width), BF16),
        scratch_shapes=[pltpu.VMEM((t, DIFF_VH), F32), pltpu.VMEM((t, DIFF_VH), F32)],
        compiler_params=pltpu.CompilerParams(dimension_semantics=("arbitrary", "arbitrary", "arbitrary")),
        name="prompt_attn",
    )(q, k, v, z, lam_params, sub_gain)


def _sample_attn_kernel(pt_ref, q_ref, kc_ref, vc_ref, kn_ref, vn_ref, z_ref, lam_ref, gain_ref,
                        o_ref, m_ref, l_ref, acc_ref, *, n_pages, n_heads, lam_init):
    p_idx = pl.program_id(1)
    n_cols = q_ref.shape[0]
    half = n_cols // 2

    @pl.when(p_idx == 0)
    def _():
        m_ref[...] = jnp.full_like(m_ref, MASK_VALUE)
        l_ref[...] = jnp.zeros_like(l_ref)
        acc_ref[...] = jnp.zeros_like(acc_ref)

    def update(kblk, vblk, keep):
        s = lax.dot_general(kblk.astype(BF16), q_ref[...], _NT, preferred_element_type=F32)
        s = jnp.where(keep, s, MASK_VALUE)
        m_old = m_ref[...]
        m_new = jnp.maximum(m_old, jnp.max(s, axis=0, keepdims=True))
        p = jnp.where(keep, jnp.exp(s - m_new), 0.0)
        alpha = jnp.exp(m_old - m_new)
        l_ref[...] = alpha * l_ref[...] + jnp.sum(p, axis=0, keepdims=True)
        acc_ref[...] = alpha * acc_ref[...] + _dot_tn(vblk, p)
        m_ref[...] = m_new

    rows = kc_ref.shape[0]
    r_i = lax.broadcasted_iota(jnp.int32, (rows, n_cols), 0)
    c_i = lax.broadcasted_iota(jnp.int32, (rows, n_cols), 1)
    update(kc_ref[...], vc_ref[...], (r_i % n_heads) == (c_i % n_heads))

    @pl.when(p_idx == n_pages - 1)
    def _():
        nrows = kn_ref.shape[0]
        r_n = lax.broadcasted_iota(jnp.int32, (nrows, n_cols), 0)
        c_n = lax.broadcasted_iota(jnp.int32, (nrows, n_cols), 1)
        keep = ((r_n % n_heads) == (c_n % n_heads)) & ((r_n // n_heads) <= ((c_n % half) // n_heads))
        update(kn_ref[...], vn_ref[...], keep)
        lam = _lambda(lam_ref, lam_init)
        ot = acc_ref[...] / l_ref[...]
        o = jnp.transpose(ot - lam * pltpu.roll(ot, half, 1))[:half]
        o_ref[...] = _subln_gate(o, gain_ref[...], z_ref[...], lam_init).astype(BF16)


def _sample_attn(page_table, qmat, cache_k, cache_v, k_new, v_new, z, lam_params, sub_gain, *, lam_init):
    b_, n_cols, _ = qmat.shape
    n_pages = page_table.shape[1]
    rows = cache_k.shape[1]
    n_heads = rows // PAGE_SIZE
    lh = k_new.shape[1]
    kern = functools.partial(_sample_attn_kernel, n_pages=n_pages, n_heads=n_heads, lam_init=lam_init)
    per_seq = lambda r: pl.BlockSpec((None, r, DIFF_VH), lambda b, p, pt: (b, 0, 0))
    page = pl.BlockSpec((None, rows, DIFF_VH), lambda b, p, pt: (pt[b, p], 0, 0))
    grid_spec = pltpu.PrefetchScalarGridSpec(
        num_scalar_prefetch=1,
        grid=(b_, n_pages),
        in_specs=[per_seq(n_cols), page, page, per_seq(lh), per_seq(lh), per_seq(lh),
                  pl.BlockSpec(lam_params.shape, lambda b, p, pt: (0, 0)),
                  pl.BlockSpec((1, DIFF_VH), lambda b, p, pt: (0, 0))],
        out_specs=per_seq(lh),
        scratch_shapes=[pltpu.VMEM((1, n_cols), F32), pltpu.VMEM((1, n_cols), F32),
                        pltpu.VMEM((DIFF_VH, n_cols), F32)],
    )
    return pl.pallas_call(
        kern,
        grid_spec=grid_spec,
        out_shape=jax.ShapeDtypeStruct((b_, lh, DIFF_VH), BF16),
        compiler_params=pltpu.CompilerParams(dimension_semantics=("arbitrary", "arbitrary")),
        name="sample_attn",
    )(page_table, qmat, cache_k, cache_v, k_new, v_new, z, lam_params, sub_gain)


def _out_kernel(og_ref, x_ref, gate_ref, w_ref, y_ref):
    y_ref[...] = x_ref[...] + gate_ref[...] * jnp.dot(og_ref[...], w_ref[...], preferred_element_type=F32)


def _out_proj(og, x, gate, w, *, tm):
    g_, t_, d = x.shape
    width = og.shape[-1]
    per_row = gate.shape[1] != 1
    return pl.pallas_call(
        _out_kernel,
        grid=(g_, t_ // tm),
        in_specs=[_rows_spec(tm, width, True), _rows_spec(tm, d, True), _rows_spec(tm, d, per_row),
                  _const_spec((width, d))],
        out_specs=_rows_spec(tm, d, True),
        out_shape=jax.ShapeDtypeStruct((g_, t_, d), F32),
        compiler_params=pltpu.CompilerParams(dimension_semantics=("arbitrary", "arbitrary")),
        name="out_proj",
    )(og, x, gate, w)


def _rotary_lane_tables(pos):
    half = ROPE_DIMS // 2
    inv_freq = ROPE_THETA ** (-jnp.arange(0, ROPE_DIMS, 2, dtype=F32) / ROPE_DIMS)
    ang = pos.astype(F32)[:, None] * inv_freq[None, :]
    cos, sin = jnp.cos(ang), jnp.sin(ang)
    n = pos.shape[0]
    rest = DIFF_DH - ROPE_DIMS
    comp = lambda parts: jnp.tile(jnp.concatenate(parts, axis=1), (1, DIFF_VH // DIFF_DH))
    cosf = comp([cos, cos, jnp.ones((n, rest), F32)])
    sin_lo = comp([-sin, jnp.zeros((n, half + rest), F32)])
    sin_hi = comp([jnp.zeros((n, half), F32), sin, jnp.zeros((n, rest), F32)])
    return cosf, sin_lo, sin_hi


def kernel(x_prompt, x_sample, state_gdn, state_conv, cache_k, cache_v, page_table, c_prompt, c_sample,
           ada_w_a, ada_b_a, norm_a, w_in_a, conv_w_a, a_log, dt_bias, gdn_out_gain, w_out_a,
           ada_w_kv, ada_b_kv, norm_kv, w_kv, k_gain,
           ada_w_b, ada_b_b, norm_b, w_in_b, q_gain, lam_params, subln_gain, w_out_b):
    bp, lp, d = x_prompt.shape
    bs, ls, _ = x_sample.shape
    hv = a_log.shape[-1]
    conv_dim = conv_w_a.shape[-1]
    n_heads = w_kv.shape[1] // 2 // DIFF_VH
    width = n_heads * DIFF_VH
    past = page_table.shape[1] * PAGE_SIZE
    lam_init = 0.8 - 0.6 * math.exp(-0.3 * 1)

    n_c = bp + bs
    c_all = jnp.concatenate([c_prompt, c_sample], axis=0)
    pad = -n_c % 16
    c_all = jnp.pad(c_all, ((0, pad), (0, 0))).astype(BF16)
    ada_w = jnp.concatenate([ada_w_a[0], ada_w_kv, ada_w_b[0]], axis=1).astype(BF16)
    ada_b = jnp.concatenate([ada_b_a[0], ada_b_kv, ada_b_b[0]])[None, :]
    mods = _ada(c_all, ada_w, ada_b)
    names = ("shift_a", "scale_a", "gate_a", "shift_kv", "scale_kv", "shift_b", "scale_b", "gate_b")
    mod = {nm: mods[:n_c, i * d:(i + 1) * d] for i, nm in enumerate(names)}

    w_in_bf = w_in_a[0].astype(BF16)
    w_out_a_bf = w_out_a[0].astype(BF16)
    w_kv_bf = w_kv.astype(BF16)
    w_in_b_bf = w_in_b[0].astype(BF16)
    w_out_b_bf = w_out_b[0].astype(BF16)
    norm_a2, norm_kv2, norm_b2 = norm_a[0][None, :], norm_kv[None, :], norm_b[0][None, :]
    k_gain2 = jnp.tile(k_gain, DIFF_VH // DIFF_DH)[None, :]
    q_gain2 = jnp.tile(q_gain[0], DIFF_VH // DIFF_DH)[None, :]
    out_gain2 = gdn_out_gain[0][None, :]
    sub_gain2 = subln_gain[0][None, :]
    lam_p = lam_params[0]
    a_log2, dt_bias2 = a_log[0][None, :], dt_bias[0][None, :]

    pm = {nm: v[:bp][:, None, :] for nm, v in mod.items()}
    tm = 256
    conv0 = jnp.zeros((bp, CONV_WIDTH - 1, conv_dim), F32)
    q, k, v, z, g, beta, conv_p = _gdn_in(x_prompt, pm["shift_a"], pm["scale_a"], norm_a2, w_in_bf, conv_w_a[0],
                                          a_log2, dt_bias2, conv0, tm=tm, stride=1)
    s0 = jnp.zeros((bp, hv, GDN_DK, GDN_DV), F32)
    og, st_p = _gdn(q, k, v, z, g, beta, s0, out_gain2)
    rope_p = [t[None] for t in _rotary_lane_tables(jnp.arange(lp, dtype=jnp.int32))]
    x1, k_p, v_p, kb, vb, qb, zb = _mid(og, x_prompt, pm["gate_a"], w_out_a_bf,
                                        pm["shift_kv"], pm["scale_kv"], norm_kv2, w_kv_bf, k_gain2,
                                        pm["shift_b"], pm["scale_b"], norm_b2, w_in_b_bf, q_gain2,
                                        *rope_p, tm=tm)
    oa = _prompt_attn(qb, kb, vb, zb, lam_p, sub_gain2, t=256, lam_init=lam_init)
    y_prompt = _out_proj(oa, x1, pm["gate_b"], w_out_b_bf, tm=tm)

    rows = bs * ls
    tmaj = lambda a: jnp.swapaxes(a, 0, 1).reshape(1, a.shape[0] * a.shape[1], a.shape[2])
    smaj = lambda a: jnp.swapaxes(a.reshape(a.shape[0] // bs, bs, a.shape[1]), 0, 1)
    sm_t = {nm: jnp.tile(v[bp:], (ls, 1))[None] for nm, v in mod.items()}
    conv0_s = tmaj(state_conv[0])
    outs = _gdn_in(tmaj(x_sample), sm_t["shift_a"], sm_t["scale_a"], norm_a2, w_in_bf, conv_w_a[0],
                   a_log2, dt_bias2, conv0_s, tm=rows, stride=bs)
    q, k, v, z, g, beta = [smaj(a[0]) for a in outs[:6]]
    conv_s = smaj(outs[6][0])
    og, st_s = _gdn(q, k, v, z, g, beta, state_gdn[0], out_gain2)
    sm = {nm: jnp.repeat(v[bp:], ls, axis=0)[None] for nm, v in mod.items()}
    pos_s = past + jnp.arange(ls, dtype=jnp.int32)
    rope_s = [jnp.tile(t, (bs, 1))[None] for t in _rotary_lane_tables(pos_s)]
    xs = x_sample.reshape(1, rows, d)
    x1s, k_s, v_s, kbs, vbs, qbs, zbs = _mid(og.reshape(1, rows, -1), xs, sm["gate_a"], w_out_a_bf,
                                             sm["shift_kv"], sm["scale_kv"], norm_kv2, w_kv_bf, k_gain2,
                                             sm["shift_b"], sm["scale_b"], norm_b2, w_in_b_bf, q_gain2,
                                             *rope_s, tm=rows)
    q5 = qbs.reshape(bs, ls, n_heads, 2, DIFF_DH)
    qc = jnp.moveaxis(q5, 3, 1)
    zeros = jnp.zeros_like(qc[:, 0])
    qmat = jnp.stack([jnp.concatenate([qc[:, 0], zeros], axis=-1),
                      jnp.concatenate([zeros, qc[:, 1]], axis=-1)], axis=1)
    qmat = qmat.reshape(bs, 2 * ls * n_heads, DIFF_VH)
    tok_head = lambda a: a.reshape(bs, ls * n_heads, DIFF_VH)
    oas = _sample_attn(page_table, qmat,
                       cache_k.reshape(cache_k.shape[0], PAGE_SIZE * n_heads, DIFF_VH),
                       cache_v.reshape(cache_v.shape[0], PAGE_SIZE * n_heads, DIFF_VH),
                       tok_head(kbs), tok_head(vbs), tok_head(zbs), lam_p, sub_gain2, lam_init=lam_init)
    y_sample = _out_proj(oas.reshape(1, rows, width), x1s, sm["gate_b"], w_out_b_bf, tm=rows)

    return (y_prompt, y_sample.reshape(bs, ls, d),
            st_p[None], conv_p[None],
            k_p.reshape(bp, lp, n_heads, DIFF_VH), v_p.reshape(bp, lp, n_heads, DIFF_VH),
            st_s[None], conv_s[None],
            k_s.reshape(bs, ls, n_heads, DIFF_VH), v_s.reshape(bs, ls, n_heads, DIFF_VH))
```

```python
import functools
import math

import jax
import jax.numpy as jnp
from jax import lax
from jax.experimental import pallas as pl
from jax.experimental.pallas import tpu as pltpu

F32 = jnp.float32
BF16 = jnp.bfloat16
NORM_EPS = 1e-6

GDN_DK = 128
GDN_DV = 128
CONV_WIDTH = 4
GDN_CHUNK = 64
TRI_LEAF = 8
HEAD_GROUP = 16
SAMPLE_PAGES_PER_STEP = 4
DIFF_DH = 64
DIFF_VH = 2 * DIFF_DH
ROPE_DIMS = DIFF_DH // 4
ROPE_THETA = 500000.0
PAGE_SIZE = 128
MASK_VALUE = -1e30

V7X_VMEM_BYTES = 64 * 1024 * 1024
SUBLANES = 8
LANES = 128

_NT = (((1,), (1,)), ((), ()))
_TN = (((0,), (0,)), ((), ()))


def _vmem_limit(nbytes):
    return int(min(nbytes + (8 << 20), V7X_VMEM_BYTES - (6 << 20)))


def _sigmoid(x):
    return 1.0 / (1.0 + jnp.exp(-x))


def _silu(x):
    return x * _sigmoid(x)


def _dot(a, b):
    return jnp.dot(a.astype(BF16), b.astype(BF16), preferred_element_type=F32)


def _dot_nt(a, b):
    return lax.dot_general(a.astype(BF16), b.astype(BF16), _NT, preferred_element_type=F32)


def _dot_tn(a, b):
    return lax.dot_general(a.astype(BF16), b.astype(BF16), _TN, preferred_element_type=F32)


def _rows_spec(tm, width, per_row):
    if per_row:
        return pl.BlockSpec((None, tm, width), lambda g, i: (g, i, 0))
    return pl.BlockSpec((None, 1, width), lambda g, i: (g, 0, 0))


def _const_spec(shape):
    return pl.BlockSpec(shape, lambda g, i: (0,) * len(shape), pipeline_mode=pl.Buffered(1))


def _ada_kernel(c_ref, w_ref, b_ref, o_ref):
    o_ref[...] = jnp.dot(c_ref[...], w_ref[...], preferred_element_type=F32) + b_ref[...]


def _ada(c, w, b):
    rows, d = c.shape
    n = w.shape[1]
    tn = 1024
    return pl.pallas_call(
        _ada_kernel,
        grid=(n // tn,),
        in_specs=[pl.BlockSpec((rows, d), lambda j: (0, 0)),
                  pl.BlockSpec((d, tn), lambda j: (0, j)),
                  pl.BlockSpec((1, tn), lambda j: (0, j))],
        out_specs=pl.BlockSpec((rows, tn), lambda j: (0, j)),
        out_shape=jax.ShapeDtypeStruct((rows, n), F32),
        name="ada",
    )(c, w, b)


def _gdn_in_kernel(x_ref, shift_ref, scale_ref, gain_ref, w_ref, cw_ref, alog_ref, dtb_ref, conv0_ref,
                   q_ref, k_ref, v_ref, z_ref, g_ref, beta_ref, convo_ref, xc_ref,
                   *, tm, stride, nt, qk_width, v_width, n_vheads):
    i = pl.program_id(1)
    conv_dim = 2 * qk_width + v_width
    hist = (CONV_WIDTH - 1) * stride
    hist_base = -(-hist // SUBLANES) * SUBLANES

    x = x_ref[...]
    h = x * lax.rsqrt(jnp.mean(x * x, axis=-1, keepdims=True) + NORM_EPS) * gain_ref[...]
    h = h * (1.0 + scale_ref[...]) + shift_ref[...]
    hb = h.astype(BF16)

    @pl.when(i == 0)
    def _():
        xc_ref[hist_base - hist:hist_base, :] = conv0_ref[...]

    n_qk_heads = qk_width // GDN_DK
    for c0 in range(0, conv_dim, qk_width):
        cols = slice(c0, c0 + qk_width)
        xc_ref[hist_base:hist_base + tm, cols] = jnp.dot(hb, w_ref[:, cols], preferred_element_type=F32)
        y = xc_ref[hist_base - hist:hist_base - hist + tm, cols] * cw_ref[0:1, cols]
        for j in range(1, CONV_WIDTH):
            r0 = hist_base - hist + j * stride
            y = y + xc_ref[r0:r0 + tm, cols] * cw_ref[j:j + 1, cols]
        a = _silu(y)
        if c0 < 2 * qk_width:
            out_ref = q_ref if c0 == 0 else k_ref
            post = GDN_DK ** -0.5 if c0 == 0 else 1.0
            for hd in range(n_qk_heads):
                ah = a[:, hd * GDN_DK:(hd + 1) * GDN_DK]
                nrm = lax.rsqrt(jnp.sum(ah * ah, axis=-1, keepdims=True) + NORM_EPS)
                out_ref[:, hd * GDN_DK:(hd + 1) * GDN_DK] = ah * nrm * post
        else:
            v0 = c0 - 2 * qk_width
            v_ref[:, v0:v0 + qk_width] = a

    new_hist = xc_ref[hist_base + tm - hist:hist_base + tm, :]
    xc_ref[hist_base - hist:hist_base, :] = new_hist

    @pl.when(i == nt - 1)
    def _():
        convo_ref[...] = new_hist

    for c0 in range(0, v_width, qk_width):
        zc = jnp.dot(hb, w_ref[:, conv_dim + c0:conv_dim + c0 + qk_width], preferred_element_type=F32)
        z_ref[:, c0:c0 + qk_width] = _silu(zc)

    g0 = conv_dim + v_width
    bg = jnp.dot(hb, w_ref[:, g0:g0 + 2 * n_vheads], preferred_element_type=F32)
    beta_ref[...] = _sigmoid(bg[:, :n_vheads])
    a_in = bg[:, n_vheads:] + dtb_ref[...]
    softplus = jnp.maximum(a_in, 0.0) + jnp.log1p(jnp.exp(-jnp.abs(a_in)))
    g_ref[...] = -jnp.exp(alog_ref[...]) * softplus


def _gdn_in(x, shift, scale, gain, w_in, conv_w, a_log, dt_bias, conv0, *, tm, stride):
    g_, t_, d = x.shape
    n_vheads = a_log.shape[-1]
    conv_dim = conv_w.shape[-1]
    v_width = n_vheads * GDN_DV
    qk_width = (conv_dim - v_width) // 2
    in_width = w_in.shape[1]
    nt = t_ // tm
    hist = (CONV_WIDTH - 1) * stride
    hist_base = -(-hist // SUBLANES) * SUBLANES
    per_row = shift.shape[1] != 1
    kern = functools.partial(_gdn_in_kernel, tm=tm, stride=stride, nt=nt, qk_width=qk_width,
                             v_width=v_width, n_vheads=n_vheads)
    est = (w_in.size * 2 + 2 * tm * d * 4 + 2 * tm * (2 * qk_width + 2 * v_width) * 4
           + (hist_base + tm) * conv_dim * 4 + 4 * tm * qk_width * 4 + (4 * tm * d * 4 if per_row else 0))
    outs = pl.pallas_call(
        kern,
        grid=(g_, nt),
        in_specs=[_rows_spec(tm, d, True), _rows_spec(tm, d, per_row), _rows_spec(tm, d, per_row),
                  _const_spec((1, d)), _const_spec((d, in_width)), _const_spec((CONV_WIDTH, conv_dim)),
                  _const_spec((1, n_vheads)), _const_spec((1, n_vheads)),
                  pl.BlockSpec((None, hist, conv_dim), lambda g, i: (g, 0, 0))],
        out_specs=[_rows_spec(tm, qk_width, True), _rows_spec(tm, qk_width, True),
                   _rows_spec(tm, v_width, True), _rows_spec(tm, v_width, True),
                   _rows_spec(tm, n_vheads, True), _rows_spec(tm, n_vheads, True),
                   pl.BlockSpec((None, hist, conv_dim), lambda g, i: (g, 0, 0))],
        out_shape=[jax.ShapeDtypeStruct((g_, t_, qk_width), F32), jax.ShapeDtypeStruct((g_, t_, qk_width), F32),
                   jax.ShapeDtypeStruct((g_, t_, v_width), F32), jax.ShapeDtypeStruct((g_, t_, v_width), F32),
                   jax.ShapeDtypeStruct((g_, t_, n_vheads), F32), jax.ShapeDtypeStruct((g_, t_, n_vheads), F32),
                   jax.ShapeDtypeStruct((g_, hist, conv_dim), F32)],
        scratch_shapes=[pltpu.VMEM((hist_base + tm, conv_dim), F32)],
        compiler_params=pltpu.CompilerParams(dimension_semantics=("arbitrary", "arbitrary"),
                                             vmem_limit_bytes=_vmem_limit(est)),
        name="gdn_in",
    )(x, shift, scale, gain, w_in, conv_w, a_log, dt_bias, conv0)
    return outs


def _gdn_kernel(q_ref, k_ref, v_ref, z_ref, g_ref, gt_ref, beta_ref, s0_ref, gain_ref,
                og_ref, sout_ref, s_ref, *, c, nc, n_qk_heads, rep):
    n = pl.program_id(1)

    @pl.when(n == 0)
    def _():
        s_ref[...] = s0_ref[...]

    row = lax.broadcasted_iota(jnp.int32, (c, c), 0)
    col = lax.broadcasted_iota(jnp.int32, (c, c), 1)
    lower = row >= col
    strict = row > col
    gc_all = jnp.dot(lower.astype(F32), g_ref[...], precision=lax.Precision.HIGHEST,
                     preferred_element_type=F32)
    gcr_all = jnp.dot(gt_ref[...], (row <= col).astype(F32), precision=lax.Precision.HIGHEST,
                      preferred_element_type=F32)
    beta_all = beta_ref[...]
    eye = (row == col).astype(F32)
    leaf = min(c, TRI_LEAF)
    same_block = lambda m: (row >> int(math.log2(m))) == (col >> int(math.log2(m)))
    leaf_mask = same_block(leaf)
    merge_masks = []
    m = leaf
    while m < c:
        merge_masks.append(same_block(2 * m) & jnp.logical_not(same_block(m)))
        m *= 2

    def unit_lower_inverses(a_list):
        ds = [jnp.where(leaf_mask, a, 0.0) for a in a_list]
        ts = [eye - d for d in ds]
        ps = ds
        for _ in range(int(math.log2(leaf)) - 1):
            ps = [_dot(p, p) for p in ps]
            ts = [t + _dot(t, p) for t, p in zip(ts, ps)]
        for mask in merge_masks:
            tmp = [_dot(jnp.where(mask, a, 0.0), t) for a, t in zip(a_list, ts)]
            ts = [t - _dot(t, x) for t, x in zip(ts, tmp)]
        return ts

    group_qk = max(1, HEAD_GROUP // rep)
    for j0 in range(0, n_qk_heads, group_qk):
        js = range(j0, j0 + group_qk)
        hvs = [j * rep + r for j in js for r in range(rep)]
        qs = {j: q_ref[:, j * GDN_DK:(j + 1) * GDN_DK] for j in js}
        ks = {j: k_ref[:, j * GDN_DK:(j + 1) * GDN_DK] for j in js}
        kks = {j: _dot_nt(ks[j], ks[j]) for j in js}
        qks = {j: _dot_nt(qs[j], ks[j]) for j in js}
        gcs = [gc_all[:, hv:hv + 1] for hv in hvs]
        betas = [beta_all[:, hv:hv + 1] for hv in hvs]
        decs = [jnp.exp(jnp.where(lower, gc - gcr_all[hv:hv + 1, :], MASK_VALUE)) for hv, gc in zip(hvs, gcs)]
        a_list = [jnp.where(strict, kks[hv // rep] * beta * dec, 0.0) for hv, beta, dec in zip(hvs, betas, decs)]
        egs = [jnp.exp(gc) for gc in gcs]
        rhss = [jnp.concatenate([v_ref[:, hv * GDN_DV:(hv + 1) * GDN_DV] * beta, ks[hv // rep] * (beta * eg)], axis=1)
                for hv, beta, eg in zip(hvs, betas, egs)]
        ts = unit_lower_inverses(a_list)
        xss = [_dot(t, rhs) for t, rhs in zip(ts, rhss)]
        ss = [s_ref[hv] for hv in hvs]
        wqs = [_dot(jnp.concatenate([xs[:, GDN_DV:], qs[hv // rep] * eg], axis=0), s)
               for hv, xs, eg, s in zip(hvs, xss, egs, ss)]
        v_news = [xs[:, :GDN_DV] - wq[:c] for xs, wq in zip(xss, wqs)]
        os_ = [wq[c:] + _dot(qks[hv // rep] * dec, v_new) for hv, wq, dec, v_new in zip(hvs, wqs, decs, v_news)]
        for hv, gc, s, v_new in zip(hvs, gcs, ss, v_news):
            g_last = gc[c - 1:c, :]
            s_ref[hv] = s * jnp.exp(g_last) + _dot_tn(ks[hv // rep] * jnp.exp(g_last - gc), v_new)
        for hv, o in zip(hvs, os_):
            on = o * lax.rsqrt(jnp.mean(o * o, axis=-1, keepdims=True) + NORM_EPS) * gain_ref[...]
            og_ref[:, hv * GDN_DV:(hv + 1) * GDN_DV] = (on * z_ref[:, hv * GDN_DV:(hv + 1) * GDN_DV]).astype(BF16)

    @pl.when(n == nc - 1)
    def _():
        sout_ref[...] = s_ref[...]


def _gdn(q, k, v, z, g, beta, s0, out_gain):
    b_, l_, qk_width = q.shape
    v_width = v.shape[-1]
    hv = g.shape[-1]
    n_qk_heads = qk_width // GDN_DK
    rep = hv // n_qk_heads
    c = min(GDN_CHUNK, l_)
    nc = l_ // c
    gt = jnp.swapaxes(g.reshape(b_, nc, c, hv), 2, 3)
    kern = functools.partial(_gdn_kernel, c=c, nc=nc, n_qk_heads=n_qk_heads, rep=rep)
    row = lambda w: pl.BlockSpec((None, c, w), lambda b, n: (b, n, 0))
    state = pl.BlockSpec((None, hv, GDN_DK, GDN_DV), lambda b, n: (b, 0, 0, 0))
    og, s_out = pl.pallas_call(
        kern,
        grid=(b_, nc),
        in_specs=[row(qk_width), row(qk_width), row(v_width), row(v_width), row(hv),
                  pl.BlockSpec((None, None, hv, c), lambda b, n: (b, n, 0, 0)), row(hv), state,
                  pl.BlockSpec((1, GDN_DV), lambda b, n: (0, 0))],
        out_specs=[row(v_width), state],
        out_shape=[jax.ShapeDtypeStruct((b_, l_, v_width), BF16),
                   jax.ShapeDtypeStruct((b_, hv, GDN_DK, GDN_DV), F32)],
        scratch_shapes=[pltpu.VMEM((hv, GDN_DK, GDN_DV), F32)],
        compiler_params=pltpu.CompilerParams(dimension_semantics=("arbitrary", "arbitrary")),
        name="gdn",
    )(q, k, v, z, g, gt, beta, s0, out_gain)
    return og, s_out


def _comp_norm_rotary(xh, gain, cosf, sin_lo, sin_hi):
    lane = lax.broadcasted_iota(jnp.int32, xh.shape, 1)
    first = lane < DIFF_DH
    sq = xh * xh
    s_a = jnp.sum(jnp.where(first, sq, 0.0), axis=-1, keepdims=True)
    s_b = jnp.sum(jnp.where(first, 0.0, sq), axis=-1, keepdims=True)
    ms = jnp.where(first, s_a, s_b) * (1.0 / DIFF_DH)
    y = xh * lax.rsqrt(ms + NORM_EPS) * gain
    half = ROPE_DIMS // 2
    return y * cosf + pltpu.roll(y, LANES - half, 1) * sin_lo + pltpu.roll(y, half, 1) * sin_hi


def _mid_kernel(og_ref, x_ref, gate_ref, wo_ref,
                shkv_ref, sckv_ref, nkv_ref, wkv_ref, kg_ref,
                shb_ref, scb_ref, nb_ref, wb_ref, qg_ref,
                cos_ref, slo_ref, shi_ref,
                x1_ref, k_ref, v_ref, kb_ref, vb_ref, q_ref, z_ref, *, n_heads):
    x1 = x_ref[...] + gate_ref[...] * jnp.dot(og_ref[...], wo_ref[...], preferred_element_type=F32)
    x1_ref[...] = x1
    r = x1 * lax.rsqrt(jnp.mean(x1 * x1, axis=-1, keepdims=True) + NORM_EPS)
    cosf, slo, shi = cos_ref[...], slo_ref[...], shi_ref[...]
    width = n_heads * DIFF_VH

    hkv = ((r * nkv_ref[...]) * (1.0 + sckv_ref[...]) + shkv_ref[...]).astype(BF16)
    kraw = jnp.dot(hkv, wkv_ref[:, :width], preferred_element_type=F32)
    for hd in range(n_heads):
        sl = slice(hd * DIFF_VH, (hd + 1) * DIFF_VH)
        kh = _comp_norm_rotary(kraw[:, sl], kg_ref[...], cosf, slo, shi)
        k_ref[:, sl] = kh
        kb_ref[:, sl] = kh.astype(BF16)
    vraw = jnp.dot(hkv, wkv_ref[:, width:], preferred_element_type=F32)
    v_ref[...] = vraw
    vb_ref[...] = vraw.astype(BF16)

    hq = ((r * nb_ref[...]) * (1.0 + scb_ref[...]) + shb_ref[...]).astype(BF16)
    qraw = jnp.dot(hq, wb_ref[:, :width], preferred_element_type=F32)
    for hd in range(n_heads):
        sl = slice(hd * DIFF_VH, (hd + 1) * DIFF_VH)
        qh = _comp_norm_rotary(qraw[:, sl], qg_ref[...], cosf, slo, shi)
        q_ref[:, sl] = (qh * DIFF_DH ** -0.5).astype(BF16)
    z_ref[...] = _silu(jnp.dot(hq, wb_ref[:, width:], preferred_element_type=F32))


def _mid(og, x, gate, w_out, shift_kv, scale_kv, norm_kv, w_kv, k_gain,
         shift_b, scale_b, norm_b, w_b, q_gain, cosf, sin_lo, sin_hi, *, tm):
    g_, t_, d = x.shape
    v_width = og.shape[-1]
    width = w_kv.shape[1] // 2
    n_heads = width // DIFF_VH
    per_row = gate.shape[1] != 1
    rope_per_group = cosf.shape[0] != 1
    rope_spec = pl.BlockSpec((None, tm, LANES), (lambda g, i: (g, i, 0)) if rope_per_group else (lambda g, i: (0, i, 0)))
    mod = _rows_spec(tm, d, per_row)
    est = ((w_out.size + w_kv.size + w_b.size) * 2 + 2 * tm * (v_width * 2 + d * 4)
           + 2 * tm * width * (4 + 4 + 4 + 2 + 2 + 2 + 4) + 8 * tm * width * 4 + (10 * tm * d * 4 if per_row else 0))
    kern = functools.partial(_mid_kernel, n_heads=n_heads)
    f32_out = jax.ShapeDtypeStruct((g_, t_, width), F32)
    bf_out = jax.ShapeDtypeStruct((g_, t_, width), BF16)
    return pl.pallas_call(
        kern,
        grid=(g_, t_ // tm),
        in_specs=[_rows_spec(tm, v_width, True), _rows_spec(tm, d, True), mod, _const_spec((v_width, d)),
                  mod, mod, _const_spec((1, d)), _const_spec((d, 2 * width)), _const_spec((1, DIFF_VH)),
                  mod, mod, _const_spec((1, d)), _const_spec((d, 2 * width)), _const_spec((1, DIFF_VH)),
                  rope_spec, rope_spec, rope_spec],
        out_specs=[_rows_spec(tm, d, True)] + [_rows_spec(tm, width, True)] * 6,
        out_shape=[jax.ShapeDtypeStruct((g_, t_, d), F32), f32_out, f32_out, bf_out, bf_out, bf_out, f32_out],
        compiler_params=pltpu.CompilerParams(dimension_semantics=("arbitrary", "arbitrary"),
                                             vmem_limit_bytes=_vmem_limit(est)),
        name="mid",
    )(og, x, gate, w_out, shift_kv, scale_kv, norm_kv, w_kv, k_gain,
      shift_b, scale_b, norm_b, w_b, q_gain, cosf, sin_lo, sin_hi)


def _lambda(lam_ref, lam_init):
    lp = lam_ref[...]
    a = jnp.sum(lp[0:1] * lp[1:2], axis=-1, keepdims=True)
    b = jnp.sum(lp[2:3] * lp[3:4], axis=-1, keepdims=True)
    return jnp.exp(a) - jnp.exp(b) + lam_init


def _subln_gate(o, gain, z, lam_init):
    on = o * lax.rsqrt(jnp.mean(o * o, axis=-1, keepdims=True) + NORM_EPS) * gain
    return (on * (1.0 - lam_init)) * z


def _prompt_attn_kernel(q_ref, k_ref, v_ref, z_ref, lam_ref, gain_ref, o_ref, acc1_ref, acc2_ref,
                        *, t, lam_init):
    i = pl.program_id(2)
    q = q_ref[...]
    lane = lax.broadcasted_iota(jnp.int32, q.shape, 1)
    zero = jnp.zeros_like(q)
    q1 = jnp.where(lane < DIFF_DH, q, zero)
    q2 = jnp.where(lane < DIFF_DH, zero, q)
    acc1_ref[...] = jnp.zeros_like(acc1_ref)
    acc2_ref[...] = jnp.zeros_like(acc2_ref)

    def update(s, m, l, acc_ref, vj):
        m_new = jnp.maximum(m, jnp.max(s, axis=-1, keepdims=True))
        p = jnp.exp(s - m_new)
        alpha = jnp.exp(m - m_new)
        acc_ref[...] = alpha * acc_ref[...] + jnp.dot(p.astype(BF16), vj, preferred_element_type=F32)
        return m_new, alpha * l + jnp.sum(p, axis=-1, keepdims=True)

    def step(j, carry, masked):
        m1, l1, m2, l2 = carry
        start = pl.multiple_of(j * t, t)
        kj = k_ref[pl.ds(start, t), :]
        vj = v_ref[pl.ds(start, t), :]
        s1 = lax.dot_general(q1, kj, _NT, preferred_element_type=F32)
        s2 = lax.dot_general(q2, kj, _NT, preferred_element_type=F32)
        if masked:
            keep = lax.broadcasted_iota(jnp.int32, (t, t), 0) >= lax.broadcasted_iota(jnp.int32, (t, t), 1)
            s1 = jnp.where(keep, s1, MASK_VALUE)
            s2 = jnp.where(keep, s2, MASK_VALUE)
        m1, l1 = update(s1, m1, l1, acc1_ref, vj)
        m2, l2 = update(s2, m2, l2, acc2_ref, vj)
        return m1, l1, m2, l2

    neg = jnp.full((t, 1), MASK_VALUE, F32)
    zer = jnp.zeros((t, 1), F32)
    carry = lax.fori_loop(0, i, lambda j, cr: step(j, cr, False), (neg, zer, neg, zer))
    m1, l1, m2, l2 = step(i, carry, True)
    lam = _lambda(lam_ref, lam_init)
    o = acc1_ref[...] / l1 - lam * (acc2_ref[...] / l2)
    o_ref[...] = _subln_gate(o, gain_ref[...], z_ref[...], lam_init).astype(BF16)


def _prompt_attn(q, k, v, z, lam_params, sub_gain, *, t, lam_init):
    b_, l_, width = q.shape
    n_heads = width // DIFF_VH
    kern = functools.partial(_prompt_attn_kernel, t=t, lam_init=lam_init)
    tile = pl.BlockSpec((None, t, DIFF_VH), lambda b, h, i: (b, i, h))
    full = pl.BlockSpec((None, l_, DIFF_VH), lambda b, h, i: (b, 0, h))
    return pl.pallas_call(
        kern,
        grid=(b_, n_heads, l_ // t),
        in_specs=[tile, full, full, tile,
                  pl.BlockSpec(lam_params.shape, lambda b, h, i: (0, 0)),
                  pl.BlockSpec((1, DIFF_VH), lambda b, h, i: (0, 0))],
        out_specs=tile,
        out_shape=jax.ShapeDtypeStruct((b_, l_, width), BF16),
        scratch_shapes=[pltpu.VMEM((t, DIFF_VH), F32), pltpu.VMEM((t, DIFF_VH), F32)],
        compiler_params=pltpu.CompilerParams(dimension_semantics=("arbitrary", "arbitrary", "arbitrary")),
        name="prompt_attn",
    )(q, k, v, z, lam_params, sub_gain)


def _sample_attn_kernel(pt_ref, q_ref, *refs, n_steps, pages_per_step, n_heads, lam_init):
    kc_refs = refs[:pages_per_step]
    vc_refs = refs[pages_per_step:2 * pages_per_step]
    kn_ref, vn_ref, z_ref, lam_ref, gain_ref, o_ref, m_ref, l_ref, acc_ref = refs[2 * pages_per_step:]
    step = pl.program_id(1)
    n_rows = q_ref.shape[0]
    half = n_rows // 2

    @pl.when(step == 0)
    def _():
        m_ref[...] = jnp.full_like(m_ref, MASK_VALUE)
        l_ref[...] = jnp.zeros_like(l_ref)
        acc_ref[...] = jnp.zeros_like(acc_ref)

    def update(k_blocks, v_blocks, keep):
        q = q_ref[...]
        ss = [jnp.where(keep, lax.dot_general(q, kb.astype(BF16), _NT, preferred_element_type=F32), MASK_VALUE)
              for kb in k_blocks]
        m_old = m_ref[...]
        m_new = m_old
        for sb in ss:
            m_new = jnp.maximum(m_new, jnp.max(sb, axis=1, keepdims=True))
        ps = [jnp.exp(sb - m_new) for sb in ss]
        alpha = jnp.exp(m_old - m_new)
        l_new = alpha * l_ref[...]
        acc = alpha * acc_ref[...]
        for pb, vb in zip(ps, v_blocks):
            l_new = l_new + jnp.sum(pb, axis=1, keepdims=True)
            acc = acc + jnp.dot(pb.astype(BF16), vb.astype(BF16), preferred_element_type=F32)
        m_ref[...] = m_new
        l_ref[...] = l_new
        acc_ref[...] = acc

    def head_match(n_keys):
        r = lax.broadcasted_iota(jnp.int32, (n_rows, n_keys), 0)
        c = lax.broadcasted_iota(jnp.int32, (n_rows, n_keys), 1)
        return r, c, (r % n_heads) == (c % n_heads)

    _, _, keep_page = head_match(kc_refs[0].shape[0])
    update([r[...] for r in kc_refs], [r[...] for r in vc_refs], keep_page)

    @pl.when(step == n_steps - 1)
    def _():
        r, c, same_head = head_match(kn_ref.shape[0])
        causal = (c // n_heads) <= ((r % half) // n_heads)
        update([kn_ref[...]], [vn_ref[...]], same_head & causal)
        lam = _lambda(lam_ref, lam_init)
        of = acc_ref[...] / l_ref[...]
        o = of[:half] - lam * of[half:]
        o_ref[...] = _subln_gate(o, gain_ref[...], z_ref[...], lam_init).astype(BF16)


def _sample_attn(page_table, qmat, cache_k, cache_v, k_new, v_new, z, lam_params, sub_gain, *, lam_init):
    b_, n_rows, _ = qmat.shape
    n_pages = page_table.shape[1]
    rows = cache_k.shape[1]
    n_heads = rows // PAGE_SIZE
    lh = z.shape[1]
    pps = SAMPLE_PAGES_PER_STEP
    n_steps = n_pages // pps
    kern = functools.partial(_sample_attn_kernel, n_steps=n_steps, pages_per_step=pps, n_heads=n_heads,
                             lam_init=lam_init)
    per_seq = lambda r: pl.BlockSpec((None, r, DIFF_VH), lambda b, s, pt: (b, 0, 0))
    page = lambda p: pl.BlockSpec((None, rows, DIFF_VH), lambda b, s, pt: (pt[b, s * pps + p], 0, 0))
    pages = [page(p) for p in range(pps)]
    grid_spec = pltpu.PrefetchScalarGridSpec(
        num_scalar_prefetch=1,
        grid=(b_, n_steps),
        in_specs=[per_seq(n_rows)] + pages + pages
                 + [per_seq(k_new.shape[1]), per_seq(v_new.shape[1]), per_seq(lh),
                    pl.BlockSpec(lam_params.shape, lambda b, s, pt: (0, 0)),
                    pl.BlockSpec((1, DIFF_VH), lambda b, s, pt: (0, 0))],
        out_specs=per_seq(lh),
        scratch_shapes=[pltpu.VMEM((n_rows, 1), F32), pltpu.VMEM((n_rows, 1), F32),
                        pltpu.VMEM((n_rows, DIFF_VH), F32)],
    )
    est = 2 * 2 * pps * rows * DIFF_VH * 4 + 8 * pps * n_rows * rows * 4
    return pl.pallas_call(
        kern,
        grid_spec=grid_spec,
        out_shape=jax.ShapeDtypeStruct((b_, lh, DIFF_VH), BF16),
        compiler_params=pltpu.CompilerParams(dimension_semantics=("arbitrary", "arbitrary"),
                                             vmem_limit_bytes=_vmem_limit(est)),
        name="sample_attn",
    )(page_table, qmat, *([cache_k] * pps), *([cache_v] * pps), k_new, v_new, z, lam_params, sub_gain)


def _out_kernel(og_ref, x_ref, gate_ref, w_ref, y_ref):
    y_ref[...] = x_ref[...] + gate_ref[...] * jnp.dot(og_ref[...], w_ref[...], preferred_element_type=F32)


def _out_proj(og, x, gate, w, *, tm):
    g_, t_, d = x.shape
    width = og.shape[-1]
    per_row = gate.shape[1] != 1
    return pl.pallas_call(
        _out_kernel,
        grid=(g_, t_ // tm),
        in_specs=[_rows_spec(tm, width, True), _rows_spec(tm, d, True), _rows_spec(tm, d, per_row),
                  _const_spec((width, d))],
        out_specs=_rows_spec(tm, d, True),
        out_shape=jax.ShapeDtypeStruct((g_, t_, d), F32),
        compiler_params=pltpu.CompilerParams(dimension_semantics=("arbitrary", "arbitrary")),
        name="out_proj",
    )(og, x, gate, w)


def _rotary_lane_tables(pos):
    half = ROPE_DIMS // 2
    inv_freq = ROPE_THETA ** (-jnp.arange(0, ROPE_DIMS, 2, dtype=F32) / ROPE_DIMS)
    ang = pos.astype(F32)[:, None] * inv_freq[None, :]
    cos, sin = jnp.cos(ang), jnp.sin(ang)
    n = pos.shape[0]
    rest = DIFF_DH - ROPE_DIMS
    comp = lambda parts: jnp.tile(jnp.concatenate(parts, axis=1), (1, DIFF_VH // DIFF_DH))
    cosf = comp([cos, cos, jnp.ones((n, rest), F32)])
    sin_lo = comp([-sin, jnp.zeros((n, half + rest), F32)])
    sin_hi = comp([jnp.zeros((n, half), F32), sin, jnp.zeros((n, rest), F32)])
    return cosf, sin_lo, sin_hi


def kernel(x_prompt, x_sample, state_gdn, state_conv, cache_k, cache_v, page_table, c_prompt, c_sample,
           ada_w_a, ada_b_a, norm_a, w_in_a, conv_w_a, a_log, dt_bias, gdn_out_gain, w_out_a,
           ada_w_kv, ada_b_kv, norm_kv, w_kv, k_gain,
           ada_w_b, ada_b_b, norm_b, w_in_b, q_gain, lam_params, subln_gain, w_out_b):
    bp, lp, d = x_prompt.shape
    bs, ls, _ = x_sample.shape
    hv = a_log.shape[-1]
    conv_dim = conv_w_a.shape[-1]
    n_heads = w_kv.shape[1] // 2 // DIFF_VH
    width = n_heads * DIFF_VH
    past = page_table.shape[1] * PAGE_SIZE
    lam_init = 0.8 - 0.6 * math.exp(-0.3 * 1)

    n_c = bp + bs
    c_all = jnp.concatenate([c_prompt, c_sample], axis=0)
    pad = -n_c % 16
    c_all = jnp.pad(c_all, ((0, pad), (0, 0))).astype(BF16)
    ada_w = jnp.concatenate([ada_w_a[0], ada_w_kv, ada_w_b[0]], axis=1).astype(BF16)
    ada_b = jnp.concatenate([ada_b_a[0], ada_b_kv, ada_b_b[0]])[None, :]
    mods = _ada(c_all, ada_w, ada_b)
    names = ("shift_a", "scale_a", "gate_a", "shift_kv", "scale_kv", "shift_b", "scale_b", "gate_b")
    mod = {nm: mods[:n_c, i * d:(i + 1) * d] for i, nm in enumerate(names)}

    w_in_bf = w_in_a[0].astype(BF16)
    w_out_a_bf = w_out_a[0].astype(BF16)
    w_kv_bf = w_kv.astype(BF16)
    w_in_b_bf = w_in_b[0].astype(BF16)
    w_out_b_bf = w_out_b[0].astype(BF16)
    norm_a2, norm_kv2, norm_b2 = norm_a[0][None, :], norm_kv[None, :], norm_b[0][None, :]
    k_gain2 = jnp.tile(k_gain, DIFF_VH // DIFF_DH)[None, :]
    q_gain2 = jnp.tile(q_gain[0], DIFF_VH // DIFF_DH)[None, :]
    out_gain2 = gdn_out_gain[0][None, :]
    sub_gain2 = subln_gain[0][None, :]
    lam_p = lam_params[0]
    a_log2, dt_bias2 = a_log[0][None, :], dt_bias[0][None, :]

    pm = {nm: v[:bp][:, None, :] for nm, v in mod.items()}
    tm = 256
    conv0 = jnp.zeros((bp, CONV_WIDTH - 1, conv_dim), F32)
    q, k, v, z, g, beta, conv_p = _gdn_in(x_prompt, pm["shift_a"], pm["scale_a"], norm_a2, w_in_bf, conv_w_a[0],
                                          a_log2, dt_bias2, conv0, tm=tm, stride=1)
    s0 = jnp.zeros((bp, hv, GDN_DK, GDN_DV), F32)
    og, st_p = _gdn(q, k, v, z, g, beta, s0, out_gain2)
    rope_p = [t[None] for t in _rotary_lane_tables(jnp.arange(lp, dtype=jnp.int32))]
    x1, k_p, v_p, kb, vb, qb, zb = _mid(og, x_prompt, pm["gate_a"], w_out_a_bf,
                                        pm["shift_kv"], pm["scale_kv"], norm_kv2, w_kv_bf, k_gain2,
                                        pm["shift_b"], pm["scale_b"], norm_b2, w_in_b_bf, q_gain2,
                                        *rope_p, tm=tm)
    oa = _prompt_attn(qb, kb, vb, zb, lam_p, sub_gain2, t=min(512, lp), lam_init=lam_init)
    y_prompt = _out_proj(oa, x1, pm["gate_b"], w_out_b_bf, tm=tm)

    rows = bs * ls
    tmaj = lambda a: jnp.swapaxes(a, 0, 1).reshape(1, a.shape[0] * a.shape[1], a.shape[2])
    smaj = lambda a: jnp.swapaxes(a.reshape(a.shape[0] // bs, bs, a.shape[1]), 0, 1)
    sm_t = {nm: jnp.tile(v[bp:], (ls, 1))[None] for nm, v in mod.items()}
    conv0_s = tmaj(state_conv[0])
    outs = _gdn_in(tmaj(x_sample), sm_t["shift_a"], sm_t["scale_a"], norm_a2, w_in_bf, conv_w_a[0],
                   a_log2, dt_bias2, conv0_s, tm=rows, stride=bs)
    q, k, v, z, g, beta = [smaj(a[0]) for a in outs[:6]]
    conv_s = smaj(outs[6][0])
    og, st_s = _gdn(q, k, v, z, g, beta, state_gdn[0], out_gain2)
    sm = {nm: jnp.repeat(v[bp:], ls, axis=0)[None] for nm, v in mod.items()}
    pos_s = past + jnp.arange(ls, dtype=jnp.int32)
    rope_s = [jnp.tile(t, (bs, 1))[None] for t in _rotary_lane_tables(pos_s)]
    xs = x_sample.reshape(1, rows, d)
    x1s, k_s, v_s, kbs, vbs, qbs, zbs = _mid(og.reshape(1, rows, -1), xs, sm["gate_a"], w_out_a_bf,
                                             sm["shift_kv"], sm["scale_kv"], norm_kv2, w_kv_bf, k_gain2,
                                             sm["shift_b"], sm["scale_b"], norm_b2, w_in_b_bf, q_gain2,
                                             *rope_s, tm=rows)
    q5 = qbs.reshape(bs, ls, n_heads, 2, DIFF_DH)
    qc = jnp.moveaxis(q5, 3, 1)
    zeros = jnp.zeros_like(qc[:, 0])
    qmat = jnp.stack([jnp.concatenate([qc[:, 0], zeros], axis=-1),
                      jnp.concatenate([zeros, qc[:, 1]], axis=-1)], axis=1)
    qmat = qmat.reshape(bs, 2 * ls * n_heads, DIFF_VH)
    tok_head = lambda a: a.reshape(bs, ls * n_heads, DIFF_VH)
    pad_keys = lambda a: jnp.pad(a, ((0, 0), (0, -a.shape[1] % LANES), (0, 0)))
    oas = _sample_attn(page_table, qmat,
                       cache_k.reshape(cache_k.shape[0], PAGE_SIZE * n_heads, DIFF_VH),
                       cache_v.reshape(cache_v.shape[0], PAGE_SIZE * n_heads, DIFF_VH),
                       pad_keys(tok_head(kbs)), pad_keys(tok_head(vbs)), tok_head(zbs), lam_p, sub_gain2,
                       lam_init=lam_init)
    y_sample = _out_proj(oas.reshape(1, rows, width), x1s, sm["gate_b"], w_out_b_bf, tm=rows)

    return (y_prompt, y_sample.reshape(bs, ls, d),
            st_p[None], conv_p[None],
            k_p.reshape(bp, lp, n_heads, DIFF_VH), v_p.reshape(bp, lp, n_heads, DIFF_VH),
            st_s[None], conv_s[None],
            k_s.reshape(bs, ls, n_heads, DIFF_VH), v_s.reshape(bs, ls, n_heads, DIFF_VH))
```

```python
import functools
import math

import jax
import jax.numpy as jnp
from jax import lax
from jax.experimental import pallas as pl
from jax.experimental.pallas import tpu as pltpu

F32 = jnp.float32
BF16 = jnp.bfloat16
NORM_EPS = 1e-6

GDN_DK = 128
GDN_DV = 128
CONV_WIDTH = 4
GDN_CHUNK = 64
TRI_LEAF = 8
HEAD_GROUP = 16
SAMPLE_PAGES_PER_STEP = 8
DIFF_DH = 64
DIFF_VH = 2 * DIFF_DH
ROPE_DIMS = DIFF_DH // 4
ROPE_THETA = 500000.0
PAGE_SIZE = 128
MASK_VALUE = -1e30
LOG2_E = math.log2(math.e)

V7X_VMEM_BYTES = 64 * 1024 * 1024
SUBLANES = 8
LANES = 128

_NT = (((1,), (1,)), ((), ()))
_TN = (((0,), (0,)), ((), ()))


def _vmem_limit(nbytes):
    return int(min(nbytes + (8 << 20), V7X_VMEM_BYTES - (6 << 20)))


def _sigmoid(x):
    return 1.0 / (1.0 + jnp.exp(-x))


def _silu(x):
    return x * _sigmoid(x)


def _dot(a, b):
    return jnp.dot(a.astype(BF16), b.astype(BF16), preferred_element_type=F32)


def _dot_nt(a, b):
    return lax.dot_general(a.astype(BF16), b.astype(BF16), _NT, preferred_element_type=F32)


def _dot_tn(a, b):
    return lax.dot_general(a.astype(BF16), b.astype(BF16), _TN, preferred_element_type=F32)


def _rows_spec(tm, width, per_row):
    if per_row:
        return pl.BlockSpec((None, tm, width), lambda g, i: (g, i, 0))
    return pl.BlockSpec((None, 1, width), lambda g, i: (g, 0, 0))


def _const_spec(shape):
    return pl.BlockSpec(shape, lambda g, i: (0,) * len(shape), pipeline_mode=pl.Buffered(1))


def _ada_kernel(c_ref, w_ref, b_ref, o_ref):
    o_ref[...] = jnp.dot(c_ref[...], w_ref[...], preferred_element_type=F32) + b_ref[...]


def _ada(c, w, b):
    rows, d = c.shape
    n = w.shape[1]
    tn = 1024
    return pl.pallas_call(
        _ada_kernel,
        grid=(n // tn,),
        in_specs=[pl.BlockSpec((rows, d), lambda j: (0, 0)),
                  pl.BlockSpec((d, tn), lambda j: (0, j)),
                  pl.BlockSpec((1, tn), lambda j: (0, j))],
        out_specs=pl.BlockSpec((rows, tn), lambda j: (0, j)),
        out_shape=jax.ShapeDtypeStruct((rows, n), F32),
        name="ada",
    )(c, w, b)


def _gdn_in_kernel(x_ref, shift_ref, scale_ref, gain_ref, w_ref, cw_ref, alog_ref, dtb_ref, conv0_ref,
                   q_ref, k_ref, v_ref, z_ref, g_ref, beta_ref, convo_ref, xc_ref,
                   *, tm, stride, nt, qk_width, v_width, n_vheads):
    i = pl.program_id(1)
    conv_dim = 2 * qk_width + v_width
    hist = (CONV_WIDTH - 1) * stride
    hist_base = -(-hist // SUBLANES) * SUBLANES

    x = x_ref[...]
    h = x * lax.rsqrt(jnp.mean(x * x, axis=-1, keepdims=True) + NORM_EPS) * gain_ref[...]
    h = h * (1.0 + scale_ref[...]) + shift_ref[...]
    hb = h.astype(BF16)

    @pl.when(i == 0)
    def _():
        xc_ref[hist_base - hist:hist_base, :] = conv0_ref[...]

    n_qk_heads = qk_width // GDN_DK
    for c0 in range(0, conv_dim, qk_width):
        cols = slice(c0, c0 + qk_width)
        xc_ref[hist_base:hist_base + tm, cols] = jnp.dot(hb, w_ref[:, cols], preferred_element_type=F32)
        y = xc_ref[hist_base - hist:hist_base - hist + tm, cols] * cw_ref[0:1, cols]
        for j in range(1, CONV_WIDTH):
            r0 = hist_base - hist + j * stride
            y = y + xc_ref[r0:r0 + tm, cols] * cw_ref[j:j + 1, cols]
        a = _silu(y)
        if c0 < 2 * qk_width:
            out_ref = q_ref if c0 == 0 else k_ref
            post = GDN_DK ** -0.5 if c0 == 0 else 1.0
            for hd in range(n_qk_heads):
                ah = a[:, hd * GDN_DK:(hd + 1) * GDN_DK]
                nrm = lax.rsqrt(jnp.sum(ah * ah, axis=-1, keepdims=True) + NORM_EPS)
                out_ref[:, hd * GDN_DK:(hd + 1) * GDN_DK] = ah * nrm * post
        else:
            v0 = c0 - 2 * qk_width
            v_ref[:, v0:v0 + qk_width] = a

    new_hist = xc_ref[hist_base + tm - hist:hist_base + tm, :]
    xc_ref[hist_base - hist:hist_base, :] = new_hist

    @pl.when(i == nt - 1)
    def _():
        convo_ref[...] = new_hist

    for c0 in range(0, v_width, qk_width):
        zc = jnp.dot(hb, w_ref[:, conv_dim + c0:conv_dim + c0 + qk_width], preferred_element_type=F32)
        z_ref[:, c0:c0 + qk_width] = _silu(zc)

    g0 = conv_dim + v_width
    bg = jnp.dot(hb, w_ref[:, g0:g0 + 2 * n_vheads], preferred_element_type=F32)
    beta_ref[...] = _sigmoid(bg[:, :n_vheads])
    a_in = bg[:, n_vheads:] + dtb_ref[...]
    softplus = jnp.maximum(a_in, 0.0) + jnp.log1p(jnp.exp(-jnp.abs(a_in)))
    g_ref[...] = -jnp.exp(alog_ref[...]) * softplus


def _gdn_in(x, shift, scale, gain, w_in, conv_w, a_log, dt_bias, conv0, *, tm, stride):
    g_, t_, d = x.shape
    n_vheads = a_log.shape[-1]
    conv_dim = conv_w.shape[-1]
    v_width = n_vheads * GDN_DV
    qk_width = (conv_dim - v_width) // 2
    in_width = w_in.shape[1]
    nt = t_ // tm
    hist = (CONV_WIDTH - 1) * stride
    hist_base = -(-hist // SUBLANES) * SUBLANES
    per_row = shift.shape[1] != 1
    kern = functools.partial(_gdn_in_kernel, tm=tm, stride=stride, nt=nt, qk_width=qk_width,
                             v_width=v_width, n_vheads=n_vheads)
    est = (w_in.size * 2 + 2 * tm * d * 4 + 2 * tm * (2 * qk_width + 2 * v_width) * 4
           + (hist_base + tm) * conv_dim * 4 + 4 * tm * qk_width * 4 + (4 * tm * d * 4 if per_row else 0))
    outs = pl.pallas_call(
        kern,
        grid=(g_, nt),
        in_specs=[_rows_spec(tm, d, True), _rows_spec(tm, d, per_row), _rows_spec(tm, d, per_row),
                  _const_spec((1, d)), _const_spec((d, in_width)), _const_spec((CONV_WIDTH, conv_dim)),
                  _const_spec((1, n_vheads)), _const_spec((1, n_vheads)),
                  pl.BlockSpec((None, hist, conv_dim), lambda g, i: (g, 0, 0))],
        out_specs=[_rows_spec(tm, qk_width, True), _rows_spec(tm, qk_width, True),
                   _rows_spec(tm, v_width, True), _rows_spec(tm, v_width, True),
                   _rows_spec(tm, n_vheads, True), _rows_spec(tm, n_vheads, True),
                   pl.BlockSpec((None, hist, conv_dim), lambda g, i: (g, 0, 0))],
        out_shape=[jax.ShapeDtypeStruct((g_, t_, qk_width), F32), jax.ShapeDtypeStruct((g_, t_, qk_width), F32),
                   jax.ShapeDtypeStruct((g_, t_, v_width), F32), jax.ShapeDtypeStruct((g_, t_, v_width), F32),
                   jax.ShapeDtypeStruct((g_, t_, n_vheads), F32), jax.ShapeDtypeStruct((g_, t_, n_vheads), F32),
                   jax.ShapeDtypeStruct((g_, hist, conv_dim), F32)],
        scratch_shapes=[pltpu.VMEM((hist_base + tm, conv_dim), F32)],
        compiler_params=pltpu.CompilerParams(dimension_semantics=("arbitrary", "arbitrary"),
                                             vmem_limit_bytes=_vmem_limit(est)),
        name="gdn_in",
    )(x, shift, scale, gain, w_in, conv_w, a_log, dt_bias, conv0)
    return outs


def _gdn_kernel(q_ref, k_ref, v_ref, z_ref, g_ref, gt_ref, beta_ref, s0_ref, gain_ref,
                og_ref, sout_ref, s_ref, *, c, nc, n_qk_heads, rep):
    n = pl.program_id(1)

    @pl.when(n == 0)
    def _():
        s_ref[...] = s0_ref[...]

    row = lax.broadcasted_iota(jnp.int32, (c, c), 0)
    col = lax.broadcasted_iota(jnp.int32, (c, c), 1)
    lower = row >= col
    strict = row > col
    gc_all = jnp.dot(lower.astype(F32), g_ref[...], precision=lax.Precision.HIGHEST,
                     preferred_element_type=F32)
    gcr_all = jnp.dot(gt_ref[...], (row <= col).astype(F32), precision=lax.Precision.HIGHEST,
                      preferred_element_type=F32)
    beta_all = beta_ref[...]
    eye = (row == col).astype(F32)
    leaf = min(c, TRI_LEAF)
    same_block = lambda m: (row >> int(math.log2(m))) == (col >> int(math.log2(m)))
    leaf_mask = same_block(leaf)
    merge_masks = []
    m = leaf
    while m < c:
        merge_masks.append(same_block(2 * m) & jnp.logical_not(same_block(m)))
        m *= 2

    def unit_lower_inverses(a_list):
        ds = [jnp.where(leaf_mask, a, 0.0) for a in a_list]
        ts = [eye - d for d in ds]
        ps = ds
        for _ in range(int(math.log2(leaf)) - 1):
            ps = [_dot(p, p) for p in ps]
            ts = [t + _dot(t, p) for t, p in zip(ts, ps)]
        for mask in merge_masks:
            tmp = [_dot(jnp.where(mask, a, 0.0), t) for a, t in zip(a_list, ts)]
            ts = [t - _dot(t, x) for t, x in zip(ts, tmp)]
        return ts

    group_qk = max(1, HEAD_GROUP // rep)
    for j0 in range(0, n_qk_heads, group_qk):
        js = range(j0, j0 + group_qk)
        hvs = [j * rep + r for j in js for r in range(rep)]
        qs = {j: q_ref[:, j * GDN_DK:(j + 1) * GDN_DK] for j in js}
        ks = {j: k_ref[:, j * GDN_DK:(j + 1) * GDN_DK] for j in js}
        kks = {j: _dot_nt(ks[j], ks[j]) for j in js}
        qks = {j: _dot_nt(qs[j], ks[j]) for j in js}
        gcs = [gc_all[:, hv:hv + 1] for hv in hvs]
        betas = [beta_all[:, hv:hv + 1] for hv in hvs]
        decs = [jnp.exp(jnp.where(lower, gc - gcr_all[hv:hv + 1, :], MASK_VALUE)) for hv, gc in zip(hvs, gcs)]
        a_list = [jnp.where(strict, kks[hv // rep] * beta * dec, 0.0) for hv, beta, dec in zip(hvs, betas, decs)]
        egs = [jnp.exp(gc) for gc in gcs]
        rhss = [jnp.concatenate([v_ref[:, hv * GDN_DV:(hv + 1) * GDN_DV] * beta, ks[hv // rep] * (beta * eg)], axis=1)
                for hv, beta, eg in zip(hvs, betas, egs)]
        ts = unit_lower_inverses(a_list)
        xss = [_dot(t, rhs) for t, rhs in zip(ts, rhss)]
        ss = [s_ref[hv] for hv in hvs]
        wqs = [_dot(jnp.concatenate([xs[:, GDN_DV:], qs[hv // rep] * eg], axis=0), s)
               for hv, xs, eg, s in zip(hvs, xss, egs, ss)]
        v_news = [xs[:, :GDN_DV] - wq[:c] for xs, wq in zip(xss, wqs)]
        os_ = [wq[c:] + _dot(qks[hv // rep] * dec, v_new) for hv, wq, dec, v_new in zip(hvs, wqs, decs, v_news)]
        for hv, gc, s, v_new in zip(hvs, gcs, ss, v_news):
            g_last = gc[c - 1:c, :]
            s_ref[hv] = s * jnp.exp(g_last) + _dot_tn(ks[hv // rep] * jnp.exp(g_last - gc), v_new)
        for hv, o in zip(hvs, os_):
            on = o * lax.rsqrt(jnp.mean(o * o, axis=-1, keepdims=True) + NORM_EPS) * gain_ref[...]
            og_ref[:, hv * GDN_DV:(hv + 1) * GDN_DV] = (on * z_ref[:, hv * GDN_DV:(hv + 1) * GDN_DV]).astype(BF16)

    @pl.when(n == nc - 1)
    def _():
        sout_ref[...] = s_ref[...]


def _gdn(q, k, v, z, g, beta, s0, out_gain):
    b_, l_, qk_width = q.shape
    v_width = v.shape[-1]
    hv = g.shape[-1]
    n_qk_heads = qk_width // GDN_DK
    rep = hv // n_qk_heads
    c = min(GDN_CHUNK, l_)
    nc = l_ // c
    gt = jnp.swapaxes(g.reshape(b_, nc, c, hv), 2, 3)
    kern = functools.partial(_gdn_kernel, c=c, nc=nc, n_qk_heads=n_qk_heads, rep=rep)
    row = lambda w: pl.BlockSpec((None, c, w), lambda b, n: (b, n, 0))
    state = pl.BlockSpec((None, hv, GDN_DK, GDN_DV), lambda b, n: (b, 0, 0, 0))
    og, s_out = pl.pallas_call(
        kern,
        grid=(b_, nc),
        in_specs=[row(qk_width), row(qk_width), row(v_width), row(v_width), row(hv),
                  pl.BlockSpec((None, None, hv, c), lambda b, n: (b, n, 0, 0)), row(hv), state,
                  pl.BlockSpec((1, GDN_DV), lambda b, n: (0, 0))],
        out_specs=[row(v_width), state],
        out_shape=[jax.ShapeDtypeStruct((b_, l_, v_width), BF16),
                   jax.ShapeDtypeStruct((b_, hv, GDN_DK, GDN_DV), F32)],
        scratch_shapes=[pltpu.VMEM((hv, GDN_DK, GDN_DV), F32)],
        compiler_params=pltpu.CompilerParams(dimension_semantics=("arbitrary", "arbitrary")),
        name="gdn",
    )(q, k, v, z, g, gt, beta, s0, out_gain)
    return og, s_out


def _comp_norm_rotary(xh, gain, cosf, sin_lo, sin_hi):
    lane = lax.broadcasted_iota(jnp.int32, xh.shape, 1)
    first = lane < DIFF_DH
    sq = xh * xh
    s_a = jnp.sum(jnp.where(first, sq, 0.0), axis=-1, keepdims=True)
    s_b = jnp.sum(jnp.where(first, 0.0, sq), axis=-1, keepdims=True)
    ms = jnp.where(first, s_a, s_b) * (1.0 / DIFF_DH)
    y = xh * lax.rsqrt(ms + NORM_EPS) * gain
    half = ROPE_DIMS // 2
    return y * cosf + pltpu.roll(y, LANES - half, 1) * sin_lo + pltpu.roll(y, half, 1) * sin_hi


def _mid_kernel(og_ref, x_ref, gate_ref, wo_ref,
                shkv_ref, sckv_ref, nkv_ref, wkv_ref, kg_ref,
                shb_ref, scb_ref, nb_ref, wb_ref, qg_ref,
                cos_ref, slo_ref, shi_ref,
                x1_ref, k_ref, v_ref, kb_ref, vb_ref, q_ref, z_ref, *, n_heads, v_transposed):
    x1 = x_ref[...] + gate_ref[...] * jnp.dot(og_ref[...], wo_ref[...], preferred_element_type=F32)
    x1_ref[...] = x1
    r = x1 * lax.rsqrt(jnp.mean(x1 * x1, axis=-1, keepdims=True) + NORM_EPS)
    cosf, slo, shi = cos_ref[...], slo_ref[...], shi_ref[...]
    width = n_heads * DIFF_VH

    hkv = ((r * nkv_ref[...]) * (1.0 + sckv_ref[...]) + shkv_ref[...]).astype(BF16)
    kraw = jnp.dot(hkv, wkv_ref[:, :width], preferred_element_type=F32)
    for hd in range(n_heads):
        sl = slice(hd * DIFF_VH, (hd + 1) * DIFF_VH)
        kh = _comp_norm_rotary(kraw[:, sl], kg_ref[...], cosf, slo, shi)
        k_ref[:, sl] = kh
        kb_ref[:, sl] = kh.astype(BF16)
    vraw = jnp.dot(hkv, wkv_ref[:, width:], preferred_element_type=F32)
    v_ref[...] = vraw
    if v_transposed:
        for hd in range(n_heads):
            vb_ref[hd] = jnp.transpose(vraw[:, hd * DIFF_VH:(hd + 1) * DIFF_VH]).astype(BF16)
    else:
        vb_ref[...] = vraw.astype(BF16)

    hq = ((r * nb_ref[...]) * (1.0 + scb_ref[...]) + shb_ref[...]).astype(BF16)
    qraw = jnp.dot(hq, wb_ref[:, :width], preferred_element_type=F32)
    for hd in range(n_heads):
        sl = slice(hd * DIFF_VH, (hd + 1) * DIFF_VH)
        qh = _comp_norm_rotary(qraw[:, sl], qg_ref[...], cosf, slo, shi)
        q_ref[:, sl] = (qh * (DIFF_DH ** -0.5 * LOG2_E)).astype(BF16)
    z_ref[...] = _silu(jnp.dot(hq, wb_ref[:, width:], preferred_element_type=F32))


def _mid(og, x, gate, w_out, shift_kv, scale_kv, norm_kv, w_kv, k_gain,
         shift_b, scale_b, norm_b, w_b, q_gain, cosf, sin_lo, sin_hi, *, tm, v_transposed):
    g_, t_, d = x.shape
    v_width = og.shape[-1]
    width = w_kv.shape[1] // 2
    n_heads = width // DIFF_VH
    per_row = gate.shape[1] != 1
    rope_per_group = cosf.shape[0] != 1
    rope_spec = pl.BlockSpec((None, tm, LANES), (lambda g, i: (g, i, 0)) if rope_per_group else (lambda g, i: (0, i, 0)))
    mod = _rows_spec(tm, d, per_row)
    est = ((w_out.size + w_kv.size + w_b.size) * 2 + 2 * tm * (v_width * 2 + d * 4)
           + 2 * tm * width * (4 + 4 + 4 + 2 + 2 + 2 + 4) + 8 * tm * width * 4 + (10 * tm * d * 4 if per_row else 0))
    kern = functools.partial(_mid_kernel, n_heads=n_heads, v_transposed=v_transposed)
    f32_out = jax.ShapeDtypeStruct((g_, t_, width), F32)
    bf_out = jax.ShapeDtypeStruct((g_, t_, width), BF16)
    rows_bf = _rows_spec(tm, width, True)
    if v_transposed:
        vb_out = jax.ShapeDtypeStruct((g_, n_heads, t_ // tm, DIFF_VH, tm), BF16)
        vb_spec = pl.BlockSpec((None, n_heads, None, DIFF_VH, tm), lambda g, i: (g, 0, i, 0, 0))
    else:
        vb_out, vb_spec = bf_out, rows_bf
    return pl.pallas_call(
        kern,
        grid=(g_, t_ // tm),
        in_specs=[_rows_spec(tm, v_width, True), _rows_spec(tm, d, True), mod, _const_spec((v_width, d)),
                  mod, mod, _const_spec((1, d)), _const_spec((d, 2 * width)), _const_spec((1, DIFF_VH)),
                  mod, mod, _const_spec((1, d)), _const_spec((d, 2 * width)), _const_spec((1, DIFF_VH)),
                  rope_spec, rope_spec, rope_spec],
        out_specs=[_rows_spec(tm, d, True), rows_bf, rows_bf, rows_bf, vb_spec, rows_bf, rows_bf],
        out_shape=[jax.ShapeDtypeStruct((g_, t_, d), F32), f32_out, f32_out, bf_out, vb_out, bf_out, f32_out],
        compiler_params=pltpu.CompilerParams(dimension_semantics=("arbitrary", "arbitrary"),
                                             vmem_limit_bytes=_vmem_limit(est)),
        name="mid",
    )(og, x, gate, w_out, shift_kv, scale_kv, norm_kv, w_kv, k_gain,
      shift_b, scale_b, norm_b, w_b, q_gain, cosf, sin_lo, sin_hi)


def _lambda(lam_ref, lam_init):
    lp = lam_ref[...]
    a = jnp.sum(lp[0:1] * lp[1:2], axis=-1, keepdims=True)
    b = jnp.sum(lp[2:3] * lp[3:4], axis=-1, keepdims=True)
    return jnp.exp(a) - jnp.exp(b) + lam_init


def _subln_gate(o, gain, z, lam_init):
    on = o * lax.rsqrt(jnp.mean(o * o, axis=-1, keepdims=True) + NORM_EPS) * gain
    return (on * (1.0 - lam_init)) * z


def _prompt_attn_kernel(q_ref, k_ref, vt_ref, z_ref, lam_ref, gain_ref, o_ref, acc_ref, *, t, lam_init):
    i = pl.program_id(2)
    q = q_ref[...]
    lane = lax.broadcasted_iota(jnp.int32, q.shape, 1)
    zero = jnp.zeros_like(q)
    qc = [jnp.where(lane < DIFF_DH, q, zero), jnp.where(lane < DIFF_DH, zero, q)]
    acc_ref[...] = jnp.zeros_like(acc_ref)
    tv = vt_ref.shape[-1]
    nv = t // tv

    def step(blocks, carry):
        ms, ls = carry[:2], carry[2:]
        ss = []
        for j, diagonal in blocks:
            kj = k_ref[pl.ds(pl.multiple_of(j * t, t), t), :]
            sb = [lax.dot_general(kj, qq, _NT, preferred_element_type=F32) for qq in qc]
            if diagonal:
                keep = lax.broadcasted_iota(jnp.int32, (t, t), 0) <= lax.broadcasted_iota(jnp.int32, (t, t), 1)
                sb = [jnp.where(keep, sc, MASK_VALUE) for sc in sb]
            ss.append(sb)
        m_new = list(ms)
        for sb in ss:
            m_new = [jnp.maximum(m, jnp.max(sc, axis=0, keepdims=True)) for m, sc in zip(m_new, sb)]
        alphas = [jnp.exp2(m - mn) for m, mn in zip(ms, m_new)]
        l_new = [a * l for a, l in zip(alphas, ls)]
        acc = jnp.concatenate(alphas, axis=1) * acc_ref[...]
        for (j, _), sb in zip(blocks, ss):
            ps = [jnp.exp2(sc - mn) for sc, mn in zip(sb, m_new)]
            l_new = [l + jnp.sum(p, axis=0, keepdims=True) for l, p in zip(l_new, ps)]
            p12 = jnp.concatenate([p.astype(BF16) for p in ps], axis=1)
            for c in range(nv):
                acc = acc + jnp.dot(vt_ref[j * nv + c], p12[c * tv:(c + 1) * tv], preferred_element_type=F32)
        acc_ref[...] = acc
        return (*m_new, *l_new)

    neg = jnp.full((1, t), MASK_VALUE, F32)
    zer = jnp.zeros((1, t), F32)
    carry = lax.fori_loop(0, i // 2, lambda p, cr: step([(2 * p, False), (2 * p + 1, False)], cr),
                          (neg, neg, zer, zer))
    _, _, l1, l2 = lax.cond(i % 2 == 1,
                            lambda cr: step([(i - 1, False), (i, True)], cr),
                            lambda cr: step([(i, True)], cr), carry)
    lam = _lambda(lam_ref, lam_init)
    acc = acc_ref[...]
    o = jnp.transpose(acc[:, :t] / l1 - lam * (acc[:, t:] / l2))
    o_ref[...] = _subln_gate(o, gain_ref[...], z_ref[...], lam_init).astype(BF16)


def _prompt_attn(q, k, vt, z, lam_params, sub_gain, *, t, lam_init):
    b_, l_, width = q.shape
    n_heads = width // DIFF_VH
    tv = vt.shape[-1]
    kern = functools.partial(_prompt_attn_kernel, t=t, lam_init=lam_init)
    tile = pl.BlockSpec((None, t, DIFF_VH), lambda b, h, i: (b, i, h))
    full = pl.BlockSpec((None, l_, DIFF_VH), lambda b, h, i: (b, 0, h))
    full_t = pl.BlockSpec((None, None, l_ // tv, DIFF_VH, tv), lambda b, h, i: (b, h, 0, 0, 0))
    return pl.pallas_call(
        kern,
        grid=(b_, n_heads, l_ // t),
        in_specs=[tile, full, full_t, tile,
                  pl.BlockSpec(lam_params.shape, lambda b, h, i: (0, 0)),
                  pl.BlockSpec((1, DIFF_VH), lambda b, h, i: (0, 0))],
        out_specs=tile,
        out_shape=jax.ShapeDtypeStruct((b_, l_, width), BF16),
        scratch_shapes=[pltpu.VMEM((DIFF_VH, 2 * t), F32)],
        compiler_params=pltpu.CompilerParams(dimension_semantics=("arbitrary", "arbitrary", "arbitrary")),
        name="prompt_attn",
    )(q, k, vt, z, lam_params, sub_gain)


def _sample_attn_kernel(pt_ref, q_ref, *refs, n_steps, pages_per_step, n_heads, lam_init):
    kc_refs = refs[:pages_per_step]
    vc_refs = refs[pages_per_step:2 * pages_per_step]
    kn_ref, vn_ref, z_ref, lam_ref, gain_ref, o_ref, m_ref, l_ref, acc_ref, bias_ref = refs[2 * pages_per_step:]
    step = pl.program_id(1)
    n_rows = q_ref.shape[0]
    half = n_rows // 2

    def head_match(n_keys):
        r = lax.broadcasted_iota(jnp.int32, (n_rows, n_keys), 0)
        c = lax.broadcasted_iota(jnp.int32, (n_rows, n_keys), 1)
        return r, c, (r % n_heads) == (c % n_heads)

    @pl.when(step == 0)
    def _():
        m_ref[...] = jnp.full_like(m_ref, MASK_VALUE)
        l_ref[...] = jnp.zeros_like(l_ref)
        acc_ref[...] = jnp.zeros_like(acc_ref)
        _, _, same_head = head_match(bias_ref.shape[1])
        bias_ref[...] = jnp.where(same_head, 0.0, MASK_VALUE)

    def update(k_blocks, v_blocks, bias):
        q = q_ref[...]
        ss = [lax.dot_general(q, kb.astype(BF16), _NT, preferred_element_type=F32) + bias
              for kb in k_blocks]
        m_old = m_ref[...]
        m_new = m_old
        for sb in ss:
            m_new = jnp.maximum(m_new, jnp.max(sb, axis=1, keepdims=True))
        ps = [jnp.exp2(sb - m_new) for sb in ss]
        alpha = jnp.exp2(m_old - m_new)
        l_new = alpha * l_ref[...]
        acc = alpha * acc_ref[...]
        for pb, vb in zip(ps, v_blocks):
            l_new = l_new + jnp.sum(pb, axis=1, keepdims=True)
            acc = acc + jnp.dot(pb.astype(BF16), vb.astype(BF16), preferred_element_type=F32)
        m_ref[...] = m_new
        l_ref[...] = l_new
        acc_ref[...] = acc

    update([r[...] for r in kc_refs], [r[...] for r in vc_refs], bias_ref[...])

    @pl.when(step == n_steps - 1)
    def _():
        r, c, same_head = head_match(kn_ref.shape[0])
        causal = (c // n_heads) <= ((r % half) // n_heads)
        update([kn_ref[...]], [vn_ref[...]], jnp.where(same_head & causal, 0.0, MASK_VALUE))
        lam = _lambda(lam_ref, lam_init)
        of = acc_ref[...] / l_ref[...]
        o = of[:half] - lam * of[half:]
        o_ref[...] = _subln_gate(o, gain_ref[...], z_ref[...], lam_init).astype(BF16)


def _sample_attn(page_table, qmat, cache_k, cache_v, k_new, v_new, z, lam_params, sub_gain, *, lam_init):
    b_, n_rows, _ = qmat.shape
    n_pages = page_table.shape[1]
    rows = cache_k.shape[1]
    n_heads = rows // PAGE_SIZE
    lh = z.shape[1]
    pps = SAMPLE_PAGES_PER_STEP
    n_steps = n_pages // pps
    kern = functools.partial(_sample_attn_kernel, n_steps=n_steps, pages_per_step=pps, n_heads=n_heads,
                             lam_init=lam_init)
    per_seq = lambda r: pl.BlockSpec((None, r, DIFF_VH), lambda b, s, pt: (b, 0, 0))
    page = lambda p: pl.BlockSpec((None, rows, DIFF_VH), lambda b, s, pt: (pt[b, s * pps + p], 0, 0))
    pages = [page(p) for p in range(pps)]
    grid_spec = pltpu.PrefetchScalarGridSpec(
        num_scalar_prefetch=1,
        grid=(b_, n_steps),
        in_specs=[per_seq(n_rows)] + pages + pages
                 + [per_seq(k_new.shape[1]), per_seq(v_new.shape[1]), per_seq(lh),
                    pl.BlockSpec(lam_params.shape, lambda b, s, pt: (0, 0)),
                    pl.BlockSpec((1, DIFF_VH), lambda b, s, pt: (0, 0))],
        out_specs=per_seq(lh),
        scratch_shapes=[pltpu.VMEM((n_rows, 1), F32), pltpu.VMEM((n_rows, 1), F32),
                        pltpu.VMEM((n_rows, DIFF_VH), F32), pltpu.VMEM((n_rows, rows), F32)],
    )
    est = 2 * 2 * pps * rows * DIFF_VH * 4 + (3 * pps + 1) * n_rows * rows * 4
    return pl.pallas_call(
        kern,
        grid_spec=grid_spec,
        out_shape=jax.ShapeDtypeStruct((b_, lh, DIFF_VH), BF16),
        compiler_params=pltpu.CompilerParams(dimension_semantics=("arbitrary", "arbitrary"),
                                             vmem_limit_bytes=_vmem_limit(est)),
        name="sample_attn",
    )(page_table, qmat, *([cache_k] * pps), *([cache_v] * pps), k_new, v_new, z, lam_params, sub_gain)


def _out_kernel(og_ref, x_ref, gate_ref, w_ref, y_ref):
    y_ref[...] = x_ref[...] + gate_ref[...] * jnp.dot(og_ref[...], w_ref[...], preferred_element_type=F32)


def _out_proj(og, x, gate, w, *, tm):
    g_, t_, d = x.shape
    width = og.shape[-1]
    per_row = gate.shape[1] != 1
    return pl.pallas_call(
        _out_kernel,
        grid=(g_, t_ // tm),
        in_specs=[_rows_spec(tm, width, True), _rows_spec(tm, d, True), _rows_spec(tm, d, per_row),
                  _const_spec((width, d))],
        out_specs=_rows_spec(tm, d, True),
        out_shape=jax.ShapeDtypeStruct((g_, t_, d), F32),
        compiler_params=pltpu.CompilerParams(dimension_semantics=("arbitrary", "arbitrary")),
        name="out_proj",
    )(og, x, gate, w)


def _rotary_lane_tables(pos):
    half = ROPE_DIMS // 2
    inv_freq = ROPE_THETA ** (-jnp.arange(0, ROPE_DIMS, 2, dtype=F32) / ROPE_DIMS)
    ang = pos.astype(F32)[:, None] * inv_freq[None, :]
    cos, sin = jnp.cos(ang), jnp.sin(ang)
    n = pos.shape[0]
    rest = DIFF_DH - ROPE_DIMS
    comp = lambda parts: jnp.tile(jnp.concatenate(parts, axis=1), (1, DIFF_VH // DIFF_DH))
    cosf = comp([cos, cos, jnp.ones((n, rest), F32)])
    sin_lo = comp([-sin, jnp.zeros((n, half + rest), F32)])
    sin_hi = comp([jnp.zeros((n, half), F32), sin, jnp.zeros((n, rest), F32)])
    return cosf, sin_lo, sin_hi


def kernel(x_prompt, x_sample, state_gdn, state_conv, cache_k, cache_v, page_table, c_prompt, c_sample,
           ada_w_a, ada_b_a, norm_a, w_in_a, conv_w_a, a_log, dt_bias, gdn_out_gain, w_out_a,
           ada_w_kv, ada_b_kv, norm_kv, w_kv, k_gain,
           ada_w_b, ada_b_b, norm_b, w_in_b, q_gain, lam_params, subln_gain, w_out_b):
    bp, lp, d = x_prompt.shape
    bs, ls, _ = x_sample.shape
    hv = a_log.shape[-1]
    conv_dim = conv_w_a.shape[-1]
    n_heads = w_kv.shape[1] // 2 // DIFF_VH
    width = n_heads * DIFF_VH
    past = page_table.shape[1] * PAGE_SIZE
    lam_init = 0.8 - 0.6 * math.exp(-0.3 * 1)

    n_c = bp + bs
    c_all = jnp.concatenate([c_prompt, c_sample], axis=0)
    pad = -n_c % 16
    c_all = jnp.pad(c_all, ((0, pad), (0, 0))).astype(BF16)
    ada_w = jnp.concatenate([ada_w_a[0], ada_w_kv, ada_w_b[0]], axis=1).astype(BF16)
    ada_b = jnp.concatenate([ada_b_a[0], ada_b_kv, ada_b_b[0]])[None, :]
    mods = _ada(c_all, ada_w, ada_b)
    names = ("shift_a", "scale_a", "gate_a", "shift_kv", "scale_kv", "shift_b", "scale_b", "gate_b")
    mod = {nm: mods[:n_c, i * d:(i + 1) * d] for i, nm in enumerate(names)}

    w_in_bf = w_in_a[0].astype(BF16)
    w_out_a_bf = w_out_a[0].astype(BF16)
    w_kv_bf = w_kv.astype(BF16)
    w_in_b_bf = w_in_b[0].astype(BF16)
    w_out_b_bf = w_out_b[0].astype(BF16)
    norm_a2, norm_kv2, norm_b2 = norm_a[0][None, :], norm_kv[None, :], norm_b[0][None, :]
    k_gain2 = jnp.tile(k_gain, DIFF_VH // DIFF_DH)[None, :]
    q_gain2 = jnp.tile(q_gain[0], DIFF_VH // DIFF_DH)[None, :]
    out_gain2 = gdn_out_gain[0][None, :]
    sub_gain2 = subln_gain[0][None, :]
    lam_p = lam_params[0]
    a_log2, dt_bias2 = a_log[0][None, :], dt_bias[0][None, :]

    pm = {nm: v[:bp][:, None, :] for nm, v in mod.items()}
    tm = 256
    conv0 = jnp.zeros((bp, CONV_WIDTH - 1, conv_dim), F32)
    q, k, v, z, g, beta, conv_p = _gdn_in(x_prompt, pm["shift_a"], pm["scale_a"], norm_a2, w_in_bf, conv_w_a[0],
                                          a_log2, dt_bias2, conv0, tm=tm, stride=1)
    s0 = jnp.zeros((bp, hv, GDN_DK, GDN_DV), F32)
    og, st_p = _gdn(q, k, v, z, g, beta, s0, out_gain2)
    rope_p = [t[None] for t in _rotary_lane_tables(jnp.arange(lp, dtype=jnp.int32))]
    x1, k_p, v_p, kb, vb, qb, zb = _mid(og, x_prompt, pm["gate_a"], w_out_a_bf,
                                        pm["shift_kv"], pm["scale_kv"], norm_kv2, w_kv_bf, k_gain2,
                                        pm["shift_b"], pm["scale_b"], norm_b2, w_in_b_bf, q_gain2,
                                        *rope_p, tm=tm, v_transposed=True)
    oa = _prompt_attn(qb, kb, vb, zb, lam_p, sub_gain2, t=min(512, lp), lam_init=lam_init)
    y_prompt = _out_proj(oa, x1, pm["gate_b"], w_out_b_bf, tm=tm)

    rows = bs * ls
    tmaj = lambda a: jnp.swapaxes(a, 0, 1).reshape(1, a.shape[0] * a.shape[1], a.shape[2])
    smaj = lambda a: jnp.swapaxes(a.reshape(a.shape[0] // bs, bs, a.shape[1]), 0, 1)
    sm_t = {nm: jnp.tile(v[bp:], (ls, 1))[None] for nm, v in mod.items()}
    conv0_s = tmaj(state_conv[0])
    outs = _gdn_in(tmaj(x_sample), sm_t["shift_a"], sm_t["scale_a"], norm_a2, w_in_bf, conv_w_a[0],
                   a_log2, dt_bias2, conv0_s, tm=rows, stride=bs)
    q, k, v, z, g, beta = [smaj(a[0]) for a in outs[:6]]
    conv_s = smaj(outs[6][0])
    og, st_s = _gdn(q, k, v, z, g, beta, state_gdn[0], out_gain2)
    sm = {nm: jnp.repeat(v[bp:], ls, axis=0)[None] for nm, v in mod.items()}
    pos_s = past + jnp.arange(ls, dtype=jnp.int32)
    rope_s = [jnp.tile(t, (bs, 1))[None] for t in _rotary_lane_tables(pos_s)]
    xs = x_sample.reshape(1, rows, d)
    x1s, k_s, v_s, kbs, vbs, qbs, zbs = _mid(og.reshape(1, rows, -1), xs, sm["gate_a"], w_out_a_bf,
                                             sm["shift_kv"], sm["scale_kv"], norm_kv2, w_kv_bf, k_gain2,
                                             sm["shift_b"], sm["scale_b"], norm_b2, w_in_b_bf, q_gain2,
                                             *rope_s, tm=rows, v_transposed=False)
    q5 = qbs.reshape(bs, ls, n_heads, 2, DIFF_DH)
    qc = jnp.moveaxis(q5, 3, 1)
    zeros = jnp.zeros_like(qc[:, 0])
    qmat = jnp.stack([jnp.concatenate([qc[:, 0], zeros], axis=-1),
                      jnp.concatenate([zeros, qc[:, 1]], axis=-1)], axis=1)
    qmat = qmat.reshape(bs, 2 * ls * n_heads, DIFF_VH)
    tok_head = lambda a: a.reshape(bs, ls * n_heads, DIFF_VH)
    pad_keys = lambda a: jnp.pad(a, ((0, 0), (0, -a.shape[1] % LANES), (0, 0)))
    oas = _sample_attn(page_table, qmat,
                       cache_k.reshape(cache_k.shape[0], PAGE_SIZE * n_heads, DIFF_VH),
                       cache_v.reshape(cache_v.shape[0], PAGE_SIZE * n_heads, DIFF_VH),
                       pad_keys(tok_head(kbs)), pad_keys(tok_head(vbs)), tok_head(zbs), lam_p, sub_gain2,
                       lam_init=lam_init)
    y_sample = _out_proj(oas.reshape(1, rows, width), x1s, sm["gate_b"], w_out_b_bf, tm=rows)

    return (y_prompt, y_sample.reshape(bs, ls, d),
            st_p[None], conv_p[None],
            k_p.reshape(bp, lp, n_heads, DIFF_VH), v_p.reshape(bp, lp, n_heads, DIFF_VH),
            st_s[None], conv_s[None],
            k_s.reshape(bs, ls, n_heads, DIFF_VH), v_s.reshape(bs, ls, n_heads, DIFF_VH))
```

```python
import functools
import math

import jax
import jax.numpy as jnp
from jax import lax
from jax.experimental import pallas as pl
from jax.experimental.pallas import tpu as pltpu

F32 = jnp.float32
BF16 = jnp.bfloat16
NORM_EPS = 1e-6

GDN_DK = 128
GDN_DV = 128
CONV_WIDTH = 4
GDN_CHUNK = 64
TRI_LEAF = 8
GDN_SEQS_PER_STEP = 2
SAMPLE_PAGES_PER_STEP = 8
DIFF_DH = 64
DIFF_VH = 2 * DIFF_DH
ROPE_DIMS = DIFF_DH // 4
ROPE_THETA = 500000.0
PAGE_SIZE = 128
MASK_VALUE = -1e30
LOG2_E = math.log2(math.e)

V7X_VMEM_BYTES = 64 * 1024 * 1024
SUBLANES = 8
LANES = 128

_NT = (((1,), (1,)), ((), ()))
_TN = (((0,), (0,)), ((), ()))


def _vmem_limit(nbytes):
    return int(min(nbytes + (8 << 20), V7X_VMEM_BYTES - (6 << 20)))


def _sigmoid(x):
    return 1.0 / (1.0 + jnp.exp(-x))


def _silu(x):
    return x * _sigmoid(x)


def _dot(a, b):
    return jnp.dot(a.astype(BF16), b.astype(BF16), preferred_element_type=F32)


def _dot_nt(a, b):
    return lax.dot_general(a.astype(BF16), b.astype(BF16), _NT, preferred_element_type=F32)


def _dot_tn(a, b):
    return lax.dot_general(a.astype(BF16), b.astype(BF16), _TN, preferred_element_type=F32)


def _rows_spec(tm, width, per_row):
    if per_row:
        return pl.BlockSpec((None, tm, width), lambda g, i: (g, i, 0))
    return pl.BlockSpec((None, 1, width), lambda g, i: (g, 0, 0))


def _const_spec(shape):
    return pl.BlockSpec(shape, lambda g, i: (0,) * len(shape), pipeline_mode=pl.Buffered(1))


def _ada_kernel(c_ref, w_ref, b_ref, o_ref):
    o_ref[...] = jnp.dot(c_ref[...], w_ref[...], preferred_element_type=F32) + b_ref[...]


def _ada(c, w, b):
    rows, d = c.shape
    n = w.shape[1]
    tn = 1024
    return pl.pallas_call(
        _ada_kernel,
        grid=(n // tn,),
        in_specs=[pl.BlockSpec((rows, d), lambda j: (0, 0)),
                  pl.BlockSpec((d, tn), lambda j: (0, j)),
                  pl.BlockSpec((1, tn), lambda j: (0, j))],
        out_specs=pl.BlockSpec((rows, tn), lambda j: (0, j)),
        out_shape=jax.ShapeDtypeStruct((rows, n), F32),
        name="ada",
    )(c, w, b)


def _gdn_in_kernel(x_ref, shift_ref, scale_ref, gain_ref, w_ref, cw_ref, alog_ref, dtb_ref, conv0_ref,
                   q_ref, k_ref, v_ref, z_ref, g_ref, beta_ref, convo_ref, xc_ref,
                   *, tm, stride, nt, qk_width, v_width, n_vheads):
    i = pl.program_id(1)
    conv_dim = 2 * qk_width + v_width
    hist = (CONV_WIDTH - 1) * stride
    hist_base = -(-hist // SUBLANES) * SUBLANES

    x = x_ref[...]
    h = x * lax.rsqrt(jnp.mean(x * x, axis=-1, keepdims=True) + NORM_EPS) * gain_ref[...]
    h = h * (1.0 + scale_ref[...]) + shift_ref[...]
    hb = h.astype(BF16)

    @pl.when(i == 0)
    def _():
        xc_ref[hist_base - hist:hist_base, :] = conv0_ref[...]

    n_qk_heads = qk_width // GDN_DK
    for c0 in range(0, conv_dim, qk_width):
        cols = slice(c0, c0 + qk_width)
        xc_ref[hist_base:hist_base + tm, cols] = jnp.dot(hb, w_ref[:, cols], preferred_element_type=F32)
        y = xc_ref[hist_base - hist:hist_base - hist + tm, cols] * cw_ref[0:1, cols]
        for j in range(1, CONV_WIDTH):
            r0 = hist_base - hist + j * stride
            y = y + xc_ref[r0:r0 + tm, cols] * cw_ref[j:j + 1, cols]
        a = _silu(y)
        if c0 < 2 * qk_width:
            out_ref = q_ref if c0 == 0 else k_ref
            post = GDN_DK ** -0.5 if c0 == 0 else 1.0
            for hd in range(n_qk_heads):
                ah = a[:, hd * GDN_DK:(hd + 1) * GDN_DK]
                nrm = lax.rsqrt(jnp.sum(ah * ah, axis=-1, keepdims=True) + NORM_EPS)
                out_ref[:, hd * GDN_DK:(hd + 1) * GDN_DK] = ah * nrm * post
        else:
            v0 = c0 - 2 * qk_width
            v_ref[:, v0:v0 + qk_width] = a

    new_hist = xc_ref[hist_base + tm - hist:hist_base + tm, :]
    xc_ref[hist_base - hist:hist_base, :] = new_hist

    @pl.when(i == nt - 1)
    def _():
        convo_ref[...] = new_hist

    for c0 in range(0, v_width, qk_width):
        zc = jnp.dot(hb, w_ref[:, conv_dim + c0:conv_dim + c0 + qk_width], preferred_element_type=F32)
        z_ref[:, c0:c0 + qk_width] = _silu(zc)

    g0 = conv_dim + v_width
    bg = jnp.dot(hb, w_ref[:, g0:g0 + 2 * n_vheads], preferred_element_type=F32)
    beta_ref[...] = _sigmoid(bg[:, :n_vheads])
    a_in = bg[:, n_vheads:] + dtb_ref[...]
    softplus = jnp.maximum(a_in, 0.0) + jnp.log1p(jnp.exp(-jnp.abs(a_in)))
    g_ref[...] = -jnp.exp(alog_ref[...]) * softplus


def _gdn_in(x, shift, scale, gain, w_in, conv_w, a_log, dt_bias, conv0, *, tm, stride):
    g_, t_, d = x.shape
    n_vheads = a_log.shape[-1]
    conv_dim = conv_w.shape[-1]
    v_width = n_vheads * GDN_DV
    qk_width = (conv_dim - v_width) // 2
    in_width = w_in.shape[1]
    nt = t_ // tm
    hist = (CONV_WIDTH - 1) * stride
    hist_base = -(-hist // SUBLANES) * SUBLANES
    per_row = shift.shape[1] != 1
    kern = functools.partial(_gdn_in_kernel, tm=tm, stride=stride, nt=nt, qk_width=qk_width,
                             v_width=v_width, n_vheads=n_vheads)
    est = (w_in.size * 2 + 2 * tm * d * 4 + 2 * tm * (2 * qk_width + 2 * v_width) * 4
           + (hist_base + tm) * conv_dim * 4 + 4 * tm * qk_width * 4 + (4 * tm * d * 4 if per_row else 0))
    outs = pl.pallas_call(
        kern,
        grid=(g_, nt),
        in_specs=[_rows_spec(tm, d, True), _rows_spec(tm, d, per_row), _rows_spec(tm, d, per_row),
                  _const_spec((1, d)), _const_spec((d, in_width)), _const_spec((CONV_WIDTH, conv_dim)),
                  _const_spec((1, n_vheads)), _const_spec((1, n_vheads)),
                  pl.BlockSpec((None, hist, conv_dim), lambda g, i: (g, 0, 0))],
        out_specs=[_rows_spec(tm, qk_width, True), _rows_spec(tm, qk_width, True),
                   _rows_spec(tm, v_width, True), _rows_spec(tm, v_width, True),
                   _rows_spec(tm, n_vheads, True), _rows_spec(tm, n_vheads, True),
                   pl.BlockSpec((None, hist, conv_dim), lambda g, i: (g, 0, 0))],
        out_shape=[jax.ShapeDtypeStruct((g_, t_, qk_width), F32), jax.ShapeDtypeStruct((g_, t_, qk_width), F32),
                   jax.ShapeDtypeStruct((g_, t_, v_width), F32), jax.ShapeDtypeStruct((g_, t_, v_width), F32),
                   jax.ShapeDtypeStruct((g_, t_, n_vheads), F32), jax.ShapeDtypeStruct((g_, t_, n_vheads), F32),
                   jax.ShapeDtypeStruct((g_, hist, conv_dim), F32)],
        scratch_shapes=[pltpu.VMEM((hist_base + tm, conv_dim), F32)],
        compiler_params=pltpu.CompilerParams(dimension_semantics=("arbitrary", "arbitrary"),
                                             vmem_limit_bytes=_vmem_limit(est)),
        name="gdn_in",
    )(x, shift, scale, gain, w_in, conv_w, a_log, dt_bias, conv0)
    return outs


def _gdn_kernel(q_ref, k_ref, v_ref, z_ref, g_ref, gt_ref, beta_ref, s0_ref, gain_ref,
                og_ref, sout_ref, s_ref, *, c, nc, nb, n_qk_heads, rep):
    n = pl.program_id(1)

    @pl.when(n == 0)
    def _():
        s_ref[...] = s0_ref[...]

    row = lax.broadcasted_iota(jnp.int32, (c, c), 0)
    col = lax.broadcasted_iota(jnp.int32, (c, c), 1)
    lower = row >= col
    strict = row > col
    eye = (row == col).astype(F32)
    leaf = min(c, TRI_LEAF)
    same_block = lambda m: (row >> int(math.log2(m))) == (col >> int(math.log2(m)))
    leaf_mask = same_block(leaf)
    merge_masks = []
    m = leaf
    while m < c:
        merge_masks.append(same_block(2 * m) & jnp.logical_not(same_block(m)))
        m *= 2

    def unit_lower_inverses(a_list):
        ds = [jnp.where(leaf_mask, a, 0.0) for a in a_list]
        ts = [eye - d for d in ds]
        ps = ds
        for _ in range(int(math.log2(leaf)) - 1):
            ps = [_dot(p, p) for p in ps]
            ts = [t + _dot(t, p) for t, p in zip(ts, ps)]
        for mask in merge_masks:
            tmp = [_dot(jnp.where(mask, a, 0.0), t) for a, t in zip(a_list, ts)]
            ts = [t - _dot(t, x) for t, x in zip(ts, tmp)]
        return ts

    seqs = range(nb)
    units = [(bi, hv) for bi in seqs for hv in range(n_qk_heads * rep)]
    qk_units = [(bi, j) for bi in seqs for j in range(n_qk_heads)]
    gc_all = [jnp.dot(lower.astype(F32), g_ref[bi], precision=lax.Precision.HIGHEST,
                      preferred_element_type=F32) for bi in seqs]
    gcr_all = [jnp.dot(gt_ref[bi], (row <= col).astype(F32), precision=lax.Precision.HIGHEST,
                       preferred_element_type=F32) for bi in seqs]
    beta_all = [beta_ref[bi] for bi in seqs]
    qs = {(bi, j): q_ref[bi, :, j * GDN_DK:(j + 1) * GDN_DK] for bi, j in qk_units}
    ks = {(bi, j): k_ref[bi, :, j * GDN_DK:(j + 1) * GDN_DK] for bi, j in qk_units}
    kks = {u: _dot_nt(ks[u], ks[u]) for u in qk_units}
    qks = {u: _dot_nt(qs[u], ks[u]) for u in qk_units}
    qk_of = lambda bi, hv: (bi, hv // rep)
    vcols = lambda hv: slice(hv * GDN_DV, (hv + 1) * GDN_DV)
    gcs = [gc_all[bi][:, hv:hv + 1] for bi, hv in units]
    betas = [beta_all[bi][:, hv:hv + 1] for bi, hv in units]
    decs = [jnp.exp(jnp.where(lower, gc - gcr_all[bi][hv:hv + 1, :], MASK_VALUE)) for (bi, hv), gc in zip(units, gcs)]
    a_list = [jnp.where(strict, kks[qk_of(*u)] * beta * dec, 0.0) for u, beta, dec in zip(units, betas, decs)]
    egs = [jnp.exp(gc) for gc in gcs]
    rhss = [jnp.concatenate([v_ref[bi, :, vcols(hv)] * beta, ks[qk_of(bi, hv)] * (beta * eg)], axis=1)
            for (bi, hv), beta, eg in zip(units, betas, egs)]
    ts = unit_lower_inverses(a_list)
    xss = [_dot(t, rhs) for t, rhs in zip(ts, rhss)]
    ss = [s_ref[bi, hv] for bi, hv in units]
    wqs = [_dot(jnp.concatenate([xs[:, GDN_DV:], qs[qk_of(*u)] * eg], axis=0), s)
           for u, xs, eg, s in zip(units, xss, egs, ss)]
    v_news = [xs[:, :GDN_DV] - wq[:c] for xs, wq in zip(xss, wqs)]
    os_ = [wq[c:] + _dot(qks[qk_of(*u)] * dec, v_new) for u, wq, dec, v_new in zip(units, wqs, decs, v_news)]
    for (bi, hv), gc, s, v_new in zip(units, gcs, ss, v_news):
        g_last = gc[c - 1:c, :]
        s_ref[bi, hv] = s * jnp.exp(g_last) + _dot_tn(ks[qk_of(bi, hv)] * jnp.exp(g_last - gc), v_new)
    for (bi, hv), o in zip(units, os_):
        on = o * lax.rsqrt(jnp.mean(o * o, axis=-1, keepdims=True) + NORM_EPS) * gain_ref[...]
        og_ref[bi, :, vcols(hv)] = (on * z_ref[bi, :, vcols(hv)]).astype(BF16)

    @pl.when(n == nc - 1)
    def _():
        sout_ref[...] = s_ref[...]


def _gdn(q, k, v, z, g, beta, s0, out_gain):
    b_, l_, qk_width = q.shape
    v_width = v.shape[-1]
    hv = g.shape[-1]
    n_qk_heads = qk_width // GDN_DK
    rep = hv // n_qk_heads
    c = min(GDN_CHUNK, l_)
    nc = l_ // c
    nb = GDN_SEQS_PER_STEP if b_ % GDN_SEQS_PER_STEP == 0 else 1
    gt = jnp.swapaxes(g.reshape(b_, nc, c, hv), 2, 3)
    kern = functools.partial(_gdn_kernel, c=c, nc=nc, nb=nb, n_qk_heads=n_qk_heads, rep=rep)
    row = lambda w: pl.BlockSpec((nb, c, w), lambda b, n: (b, n, 0))
    state = pl.BlockSpec((nb, hv, GDN_DK, GDN_DV), lambda b, n: (b, 0, 0, 0))
    og, s_out = pl.pallas_call(
        kern,
        grid=(b_ // nb, nc),
        in_specs=[row(qk_width), row(qk_width), row(v_width), row(v_width), row(hv),
                  pl.BlockSpec((nb, None, hv, c), lambda b, n: (b, n, 0, 0)), row(hv), state,
                  pl.BlockSpec((1, GDN_DV), lambda b, n: (0, 0))],
        out_specs=[row(v_width), state],
        out_shape=[jax.ShapeDtypeStruct((b_, l_, v_width), BF16),
                   jax.ShapeDtypeStruct((b_, hv, GDN_DK, GDN_DV), F32)],
        scratch_shapes=[pltpu.VMEM((nb, hv, GDN_DK, GDN_DV), F32)],
        compiler_params=pltpu.CompilerParams(dimension_semantics=("arbitrary", "arbitrary")),
        name="gdn",
    )(q, k, v, z, g, gt, beta, s0, out_gain)
    return og, s_out


def _comp_norm_rotary(xh, gain, cosf, sin_lo, sin_hi):
    lane = lax.broadcasted_iota(jnp.int32, xh.shape, 1)
    first = lane < DIFF_DH
    sq = xh * xh
    s_a = jnp.sum(jnp.where(first, sq, 0.0), axis=-1, keepdims=True)
    s_b = jnp.sum(jnp.where(first, 0.0, sq), axis=-1, keepdims=True)
    ms = jnp.where(first, s_a, s_b) * (1.0 / DIFF_DH)
    y = xh * lax.rsqrt(ms + NORM_EPS) * gain
    half = ROPE_DIMS // 2
    return y * cosf + pltpu.roll(y, LANES - half, 1) * sin_lo + pltpu.roll(y, half, 1) * sin_hi


def _mid_kernel(og_ref, x_ref, gate_ref, wo_ref,
                shkv_ref, sckv_ref, nkv_ref, wkv_ref, kg_ref,
                shb_ref, scb_ref, nb_ref, wb_ref, qg_ref,
                cos_ref, slo_ref, shi_ref,
                x1_ref, k_ref, v_ref, kb_ref, vb_ref, q_ref, z_ref, *, n_heads, v_transposed):
    x1 = x_ref[...] + gate_ref[...] * jnp.dot(og_ref[...], wo_ref[...], preferred_element_type=F32)
    x1_ref[...] = x1
    r = x1 * lax.rsqrt(jnp.mean(x1 * x1, axis=-1, keepdims=True) + NORM_EPS)
    cosf, slo, shi = cos_ref[...], slo_ref[...], shi_ref[...]
    width = n_heads * DIFF_VH

    hkv = ((r * nkv_ref[...]) * (1.0 + sckv_ref[...]) + shkv_ref[...]).astype(BF16)
    kraw = jnp.dot(hkv, wkv_ref[:, :width], preferred_element_type=F32)
    for hd in range(n_heads):
        sl = slice(hd * DIFF_VH, (hd + 1) * DIFF_VH)
        kh = _comp_norm_rotary(kraw[:, sl], kg_ref[...], cosf, slo, shi)
        k_ref[:, sl] = kh
        kb_ref[:, sl] = kh.astype(BF16)
    vraw = jnp.dot(hkv, wkv_ref[:, width:], preferred_element_type=F32)
    v_ref[...] = vraw
    if v_transposed:
        for hd in range(n_heads):
            vb_ref[hd] = jnp.transpose(vraw[:, hd * DIFF_VH:(hd + 1) * DIFF_VH]).astype(BF16)
    else:
        vb_ref[...] = vraw.astype(BF16)

    hq = ((r * nb_ref[...]) * (1.0 + scb_ref[...]) + shb_ref[...]).astype(BF16)
    qraw = jnp.dot(hq, wb_ref[:, :width], preferred_element_type=F32)
    for hd in range(n_heads):
        sl = slice(hd * DIFF_VH, (hd + 1) * DIFF_VH)
        qh = _comp_norm_rotary(qraw[:, sl], qg_ref[...], cosf, slo, shi)
        q_ref[:, sl] = (qh * (DIFF_DH ** -0.5 * LOG2_E)).astype(BF16)
    z_ref[...] = _silu(jnp.dot(hq, wb_ref[:, width:], preferred_element_type=F32))


def _mid(og, x, gate, w_out, shift_kv, scale_kv, norm_kv, w_kv, k_gain,
         shift_b, scale_b, norm_b, w_b, q_gain, cosf, sin_lo, sin_hi, *, tm, v_transposed):
    g_, t_, d = x.shape
    v_width = og.shape[-1]
    width = w_kv.shape[1] // 2
    n_heads = width // DIFF_VH
    per_row = gate.shape[1] != 1
    rope_per_group = cosf.shape[0] != 1
    rope_spec = pl.BlockSpec((None, tm, LANES), (lambda g, i: (g, i, 0)) if rope_per_group else (lambda g, i: (0, i, 0)))
    mod = _rows_spec(tm, d, per_row)
    est = ((w_out.size + w_kv.size + w_b.size) * 2 + 2 * tm * (v_width * 2 + d * 4)
           + 2 * tm * width * (4 + 4 + 4 + 2 + 2 + 2 + 4) + 8 * tm * width * 4 + (10 * tm * d * 4 if per_row else 0))
    kern = functools.partial(_mid_kernel, n_heads=n_heads, v_transposed=v_transposed)
    f32_out = jax.ShapeDtypeStruct((g_, t_, width), F32)
    bf_out = jax.ShapeDtypeStruct((g_, t_, width), BF16)
    rows_bf = _rows_spec(tm, width, True)
    if v_transposed:
        vb_out = jax.ShapeDtypeStruct((g_, n_heads, t_ // tm, DIFF_VH, tm), BF16)
        vb_spec = pl.BlockSpec((None, n_heads, None, DIFF_VH, tm), lambda g, i: (g, 0, i, 0, 0))
    else:
        vb_out, vb_spec = bf_out, rows_bf
    return pl.pallas_call(
        kern,
        grid=(g_, t_ // tm),
        in_specs=[_rows_spec(tm, v_width, True), _rows_spec(tm, d, True), mod, _const_spec((v_width, d)),
                  mod, mod, _const_spec((1, d)), _const_spec((d, 2 * width)), _const_spec((1, DIFF_VH)),
                  mod, mod, _const_spec((1, d)), _const_spec((d, 2 * width)), _const_spec((1, DIFF_VH)),
                  rope_spec, rope_spec, rope_spec],
        out_specs=[_rows_spec(tm, d, True), rows_bf, rows_bf, rows_bf, vb_spec, rows_bf, rows_bf],
        out_shape=[jax.ShapeDtypeStruct((g_, t_, d), F32), f32_out, f32_out, bf_out, vb_out, bf_out, f32_out],
        compiler_params=pltpu.CompilerParams(dimension_semantics=("arbitrary", "arbitrary"),
                                             vmem_limit_bytes=_vmem_limit(est)),
        name="mid",
    )(og, x, gate, w_out, shift_kv, scale_kv, norm_kv, w_kv, k_gain,
      shift_b, scale_b, norm_b, w_b, q_gain, cosf, sin_lo, sin_hi)


def _lambda(lam_ref, lam_init):
    lp = lam_ref[...]
    a = jnp.sum(lp[0:1] * lp[1:2], axis=-1, keepdims=True)
    b = jnp.sum(lp[2:3] * lp[3:4], axis=-1, keepdims=True)
    return jnp.exp(a) - jnp.exp(b) + lam_init


def _subln_gate(o, gain, z, lam_init):
    on = o * lax.rsqrt(jnp.mean(o * o, axis=-1, keepdims=True) + NORM_EPS) * gain
    return (on * (1.0 - lam_init)) * z


def _prompt_attn_kernel(q_ref, k_ref, vt_ref, z_ref, lam_ref, gain_ref, o_ref, acc_ref, sa_ref, sb_ref,
                        *, t, lam_init):
    i = pl.program_id(2)
    q = q_ref[...]
    lane = lax.broadcasted_iota(jnp.int32, q.shape, 1)
    zero = jnp.zeros_like(q)
    qc = [jnp.where(lane < DIFF_DH, q, zero), jnp.where(lane < DIFF_DH, zero, q)]
    acc_ref[...] = jnp.zeros_like(acc_ref)
    tv = vt_ref.shape[-1]
    nv = t // tv

    def scores(j):
        kj = k_ref[pl.ds(pl.multiple_of(j * t, t), t), :]
        return [lax.dot_general(kj, qq, _NT, preferred_element_type=F32) for qq in qc]

    def absorb(j, ss, stats):
        ms, ls = stats[:2], stats[2:]
        m_new = [jnp.maximum(m, jnp.max(sc, axis=0, keepdims=True)) for m, sc in zip(ms, ss)]
        ps = [jnp.exp2(sc - mn) for sc, mn in zip(ss, m_new)]
        alphas = [jnp.exp2(m - mn) for m, mn in zip(ms, m_new)]
        l_new = [a * l + jnp.sum(p, axis=0, keepdims=True) for a, l, p in zip(alphas, ls, ps)]
        p12 = jnp.concatenate([p.astype(BF16) for p in ps], axis=1)
        acc = jnp.concatenate(alphas, axis=1) * acc_ref[...]
        for c in range(nv):
            acc = acc + jnp.dot(vt_ref[j * nv + c], p12[c * tv:(c + 1) * tv], preferred_element_type=F32)
        acc_ref[...] = acc
        return (*m_new, *l_new)

    def put(buf, ss):
        buf[:, :t], buf[:, t:] = ss

    def get(buf):
        return [buf[:, :t], buf[:, t:]]

    neg = jnp.full((1, t), MASK_VALUE, F32)
    zer = jnp.zeros((1, t), F32)

    put(sa_ref, scores(0))

    def produce_and_absorb(cur, nxt, j, stats):
        put(nxt, scores(j + 1))
        return absorb(j, get(cur), stats)

    def body(j, stats):
        return lax.cond(j % 2 == 0,
                        lambda st: produce_and_absorb(sa_ref, sb_ref, j, st),
                        lambda st: produce_and_absorb(sb_ref, sa_ref, j, st), stats)

    stats = lax.fori_loop(0, i, body, (neg, neg, zer, zer))
    keep = lax.broadcasted_iota(jnp.int32, (t, t), 0) <= lax.broadcasted_iota(jnp.int32, (t, t), 1)
    diagonal = lambda buf: (lambda st: absorb(i, [jnp.where(keep, sc, MASK_VALUE) for sc in get(buf)], st))
    _, _, l1, l2 = lax.cond(i % 2 == 0, diagonal(sa_ref), diagonal(sb_ref), stats)
    lam = _lambda(lam_ref, lam_init)
    acc = acc_ref[...]
    o = jnp.transpose(acc[:, :t] / l1 - lam * (acc[:, t:] / l2))
    o_ref[...] = _subln_gate(o, gain_ref[...], z_ref[...], lam_init).astype(BF16)


def _prompt_attn(q, k, vt, z, lam_params, sub_gain, *, t, lam_init):
    b_, l_, width = q.shape
    n_heads = width // DIFF_VH
    tv = vt.shape[-1]
    kern = functools.partial(_prompt_attn_kernel, t=t, lam_init=lam_init)
    tile = pl.BlockSpec((None, t, DIFF_VH), lambda b, h, i: (b, i, h))
    full = pl.BlockSpec((None, l_, DIFF_VH), lambda b, h, i: (b, 0, h))
    full_t = pl.BlockSpec((None, None, l_ // tv, DIFF_VH, tv), lambda b, h, i: (b, h, 0, 0, 0))
    return pl.pallas_call(
        kern,
        grid=(b_, n_heads, l_ // t),
        in_specs=[tile, full, full_t, tile,
                  pl.BlockSpec(lam_params.shape, lambda b, h, i: (0, 0)),
                  pl.BlockSpec((1, DIFF_VH), lambda b, h, i: (0, 0))],
        out_specs=tile,
        out_shape=jax.ShapeDtypeStruct((b_, l_, width), BF16),
        scratch_shapes=[pltpu.VMEM((DIFF_VH, 2 * t), F32), pltpu.VMEM((t, 2 * t), F32),
                        pltpu.VMEM((t, 2 * t), F32)],
        compiler_params=pltpu.CompilerParams(dimension_semantics=("arbitrary", "arbitrary", "arbitrary")),
        name="prompt_attn",
    )(q, k, vt, z, lam_params, sub_gain)


def _sample_attn_kernel(pt_ref, q_ref, *refs, n_steps, pages_per_step, n_heads, lam_init):
    kc_refs = refs[:pages_per_step]
    vc_refs = refs[pages_per_step:2 * pages_per_step]
    kn_ref, vn_ref, z_ref, lam_ref, gain_ref, o_ref, m_ref, l_ref, acc_ref, bias_ref = refs[2 * pages_per_step:]
    step = pl.program_id(1)
    n_rows = q_ref.shape[0]
    half = n_rows // 2

    def head_match(n_keys):
        r = lax.broadcasted_iota(jnp.int32, (n_rows, n_keys), 0)
        c = lax.broadcasted_iota(jnp.int32, (n_rows, n_keys), 1)
        return r, c, (r % n_heads) == (c % n_heads)

    @pl.when(step == 0)
    def _():
        m_ref[...] = jnp.full_like(m_ref, MASK_VALUE)
        l_ref[...] = jnp.zeros_like(l_ref)
        acc_ref[...] = jnp.zeros_like(acc_ref)
        _, _, same_head = head_match(bias_ref.shape[1])
        bias_ref[...] = jnp.where(same_head, 0.0, MASK_VALUE)

    def update(k_blocks, v_blocks, bias):
        q = q_ref[...]
        ss = [lax.dot_general(q, kb.astype(BF16), _NT, preferred_element_type=F32) + bias
              for kb in k_blocks]
        m_old = m_ref[...]
        m_new = m_old
        for sb in ss:
            m_new = jnp.maximum(m_new, jnp.max(sb, axis=1, keepdims=True))
        ps = [jnp.exp2(sb - m_new) for sb in ss]
        alpha = jnp.exp2(m_old - m_new)
        l_new = alpha * l_ref[...]
        acc = alpha * acc_ref[...]
        for pb, vb in zip(ps, v_blocks):
            l_new = l_new + jnp.sum(pb, axis=1, keepdims=True)
            acc = acc + jnp.dot(pb.astype(BF16), vb.astype(BF16), preferred_element_type=F32)
        m_ref[...] = m_new
        l_ref[...] = l_new
        acc_ref[...] = acc

    update([r[...] for r in kc_refs], [r[...] for r in vc_refs], bias_ref[...])

    @pl.when(step == n_steps - 1)
    def _():
        r, c, same_head = head_match(kn_ref.shape[0])
        causal = (c // n_heads) <= ((r % half) // n_heads)
        update([kn_ref[...]], [vn_ref[...]], jnp.where(same_head & causal, 0.0, MASK_VALUE))
        lam = _lambda(lam_ref, lam_init)
        of = acc_ref[...] / l_ref[...]
        o = of[:half] - lam * of[half:]
        o_ref[...] = _subln_gate(o, gain_ref[...], z_ref[...], lam_init).astype(BF16)


def _sample_attn(page_table, qmat, cache_k, cache_v, k_new, v_new, z, lam_params, sub_gain, *, lam_init):
    b_, n_rows, _ = qmat.shape
    n_pages = page_table.shape[1]
    rows = cache_k.shape[1]
    n_heads = rows // PAGE_SIZE
    lh = z.shape[1]
    pps = SAMPLE_PAGES_PER_STEP
    n_steps = n_pages // pps
    kern = functools.partial(_sample_attn_kernel, n_steps=n_steps, pages_per_step=pps, n_heads=n_heads,
                             lam_init=lam_init)
    per_seq = lambda r: pl.BlockSpec((None, r, DIFF_VH), lambda b, s, pt: (b, 0, 0))
    page = lambda p: pl.BlockSpec((None, rows, DIFF_VH), lambda b, s, pt: (pt[b, s * pps + p], 0, 0))
    pages = [page(p) for p in range(pps)]
    grid_spec = pltpu.PrefetchScalarGridSpec(
        num_scalar_prefetch=1,
        grid=(b_, n_steps),
        in_specs=[per_seq(n_rows)] + pages + pages
                 + [per_seq(k_new.shape[1]), per_seq(v_new.shape[1]), per_seq(lh),
                    pl.BlockSpec(lam_params.shape, lambda b, s, pt: (0, 0)),
                    pl.BlockSpec((1, DIFF_VH), lambda b, s, pt: (0, 0))],
        out_specs=per_seq(lh),
        scratch_shapes=[pltpu.VMEM((n_rows, 1), F32), pltpu.VMEM((n_rows, 1), F32),
                        pltpu.VMEM((n_rows, DIFF_VH), F32), pltpu.VMEM((n_rows, rows), F32)],
    )
    est = 2 * 2 * pps * rows * DIFF_VH * 4 + (3 * pps + 1) * n_rows * rows * 4
    return pl.pallas_call(
        kern,
        grid_spec=grid_spec,
        out_shape=jax.ShapeDtypeStruct((b_, lh, DIFF_VH), BF16),
        compiler_params=pltpu.CompilerParams(dimension_semantics=("arbitrary", "arbitrary"),
                                             vmem_limit_bytes=_vmem_limit(est)),
        name="sample_attn",
    )(page_table, qmat, *([cache_k] * pps), *([cache_v] * pps), k_new, v_new, z, lam_params, sub_gain)


def _out_kernel(og_ref, x_ref, gate_ref, w_ref, y_ref):
    y_ref[...] = x_ref[...] + gate_ref[...] * jnp.dot(og_ref[...], w_ref[...], preferred_element_type=F32)


def _out_proj(og, x, gate, w, *, tm):
    g_, t_, d = x.shape
    width = og.shape[-1]
    per_row = gate.shape[1] != 1
    return pl.pallas_call(
        _out_kernel,
        grid=(g_, t_ // tm),
        in_specs=[_rows_spec(tm, width, True), _rows_spec(tm, d, True), _rows_spec(tm, d, per_row),
                  _const_spec((width, d))],
        out_specs=_rows_spec(tm, d, True),
        out_shape=jax.ShapeDtypeStruct((g_, t_, d), F32),
        compiler_params=pltpu.CompilerParams(dimension_semantics=("arbitrary", "arbitrary")),
        name="out_proj",
    )(og, x, gate, w)


def _rotary_lane_tables(pos):
    half = ROPE_DIMS // 2
    inv_freq = ROPE_THETA ** (-jnp.arange(0, ROPE_DIMS, 2, dtype=F32) / ROPE_DIMS)
    ang = pos.astype(F32)[:, None] * inv_freq[None, :]
    cos, sin = jnp.cos(ang), jnp.sin(ang)
    n = pos.shape[0]
    rest = DIFF_DH - ROPE_DIMS
    comp = lambda parts: jnp.tile(jnp.concatenate(parts, axis=1), (1, DIFF_VH // DIFF_DH))
    cosf = comp([cos, cos, jnp.ones((n, rest), F32)])
    sin_lo = comp([-sin, jnp.zeros((n, half + rest), F32)])
    sin_hi = comp([jnp.zeros((n, half), F32), sin, jnp.zeros((n, rest), F32)])
    return cosf, sin_lo, sin_hi


def kernel(x_prompt, x_sample, state_gdn, state_conv, cache_k, cache_v, page_table, c_prompt, c_sample,
           ada_w_a, ada_b_a, norm_a, w_in_a, conv_w_a, a_log, dt_bias, gdn_out_gain, w_out_a,
           ada_w_kv, ada_b_kv, norm_kv, w_kv, k_gain,
           ada_w_b, ada_b_b, norm_b, w_in_b, q_gain, lam_params, subln_gain, w_out_b):
    bp, lp, d = x_prompt.shape
    bs, ls, _ = x_sample.shape
    hv = a_log.shape[-1]
    conv_dim = conv_w_a.shape[-1]
    n_heads = w_kv.shape[1] // 2 // DIFF_VH
    width = n_heads * DIFF_VH
    past = page_table.shape[1] * PAGE_SIZE
    lam_init = 0.8 - 0.6 * math.exp(-0.3 * 1)

    n_c = bp + bs
    c_all = jnp.concatenate([c_prompt, c_sample], axis=0)
    pad = -n_c % 16
    c_all = jnp.pad(c_all, ((0, pad), (0, 0))).astype(BF16)
    ada_w = jnp.concatenate([ada_w_a[0], ada_w_kv, ada_w_b[0]], axis=1).astype(BF16)
    ada_b = jnp.concatenate([ada_b_a[0], ada_b_kv, ada_b_b[0]])[None, :]
    mods = _ada(c_all, ada_w, ada_b)
    names = ("shift_a", "scale_a", "gate_a", "shift_kv", "scale_kv", "shift_b", "scale_b", "gate_b")
    mod = {nm: mods[:n_c, i * d:(i + 1) * d] for i, nm in enumerate(names)}

    w_in_bf = w_in_a[0].astype(BF16)
    w_out_a_bf = w_out_a[0].astype(BF16)
    w_kv_bf = w_kv.astype(BF16)
    w_in_b_bf = w_in_b[0].astype(BF16)
    w_out_b_bf = w_out_b[0].astype(BF16)
    norm_a2, norm_kv2, norm_b2 = norm_a[0][None, :], norm_kv[None, :], norm_b[0][None, :]
    k_gain2 = jnp.tile(k_gain, DIFF_VH // DIFF_DH)[None, :]
    q_gain2 = jnp.tile(q_gain[0], DIFF_VH // DIFF_DH)[None, :]
    out_gain2 = gdn_out_gain[0][None, :]
    sub_gain2 = subln_gain[0][None, :]
    lam_p = lam_params[0]
    a_log2, dt_bias2 = a_log[0][None, :], dt_bias[0][None, :]

    pm = {nm: v[:bp][:, None, :] for nm, v in mod.items()}
    tm = 256
    conv0 = jnp.zeros((bp, CONV_WIDTH - 1, conv_dim), F32)
    q, k, v, z, g, beta, conv_p = _gdn_in(x_prompt, pm["shift_a"], pm["scale_a"], norm_a2, w_in_bf, conv_w_a[0],
                                          a_log2, dt_bias2, conv0, tm=tm, stride=1)
    s0 = jnp.zeros((bp, hv, GDN_DK, GDN_DV), F32)
    og, st_p = _gdn(q, k, v, z, g, beta, s0, out_gain2)
    rope_p = [t[None] for t in _rotary_lane_tables(jnp.arange(lp, dtype=jnp.int32))]
    x1, k_p, v_p, kb, vb, qb, zb = _mid(og, x_prompt, pm["gate_a"], w_out_a_bf,
                                        pm["shift_kv"], pm["scale_kv"], norm_kv2, w_kv_bf, k_gain2,
                                        pm["shift_b"], pm["scale_b"], norm_b2, w_in_b_bf, q_gain2,
                                        *rope_p, tm=tm, v_transposed=True)
    oa = _prompt_attn(qb, kb, vb, zb, lam_p, sub_gain2, t=min(512, lp), lam_init=lam_init)
    y_prompt = _out_proj(oa, x1, pm["gate_b"], w_out_b_bf, tm=tm)

    rows = bs * ls
    tmaj = lambda a: jnp.swapaxes(a, 0, 1).reshape(1, a.shape[0] * a.shape[1], a.shape[2])
    smaj = lambda a: jnp.swapaxes(a.reshape(a.shape[0] // bs, bs, a.shape[1]), 0, 1)
    sm_t = {nm: jnp.tile(v[bp:], (ls, 1))[None] for nm, v in mod.items()}
    conv0_s = tmaj(state_conv[0])
    outs = _gdn_in(tmaj(x_sample), sm_t["shift_a"], sm_t["scale_a"], norm_a2, w_in_bf, conv_w_a[0],
                   a_log2, dt_bias2, conv0_s, tm=rows, stride=bs)
    q, k, v, z, g, beta = [smaj(a[0]) for a in outs[:6]]
    conv_s = smaj(outs[6][0])
    og, st_s = _gdn(q, k, v, z, g, beta, state_gdn[0], out_gain2)
    sm = {nm: jnp.repeat(v[bp:], ls, axis=0)[None] for nm, v in mod.items()}
    pos_s = past + jnp.arange(ls, dtype=jnp.int32)
    rope_s = [jnp.tile(t, (bs, 1))[None] for t in _rotary_lane_tables(pos_s)]
    xs = x_sample.reshape(1, rows, d)
    x1s, k_s, v_s, kbs, vbs, qbs, zbs = _mid(og.reshape(1, rows, -1), xs, sm["gate_a"], w_out_a_bf,
                                             sm["shift_kv"], sm["scale_kv"], norm_kv2, w_kv_bf, k_gain2,
                                             sm["shift_b"], sm["scale_b"], norm_b2, w_in_b_bf, q_gain2,
                                             *rope_s, tm=rows, v_transposed=False)
    q5 = qbs.reshape(bs, ls, n_heads, 2, DIFF_DH)
    qc = jnp.moveaxis(q5, 3, 1)
    zeros = jnp.zeros_like(qc[:, 0])
    qmat = jnp.stack([jnp.concatenate([qc[:, 0], zeros], axis=-1),
                      jnp.concatenate([zeros, qc[:, 1]], axis=-1)], axis=1)
    qmat = qmat.reshape(bs, 2 * ls * n_heads, DIFF_VH)
    tok_head = lambda a: a.reshape(bs, ls * n_heads, DIFF_VH)
    pad_keys = lambda a: jnp.pad(a, ((0, 0), (0, -a.shape[1] % LANES), (0, 0)))
    oas = _sample_attn(page_table, qmat,
                       cache_k.reshape(cache_k.shape[0], PAGE_SIZE * n_heads, DIFF_VH),
                       cache_v.reshape(cache_v.shape[0], PAGE_SIZE * n_heads, DIFF_VH),
                       pad_keys(tok_head(kbs)), pad_keys(tok_head(vbs)), tok_head(zbs), lam_p, sub_gain2,
                       lam_init=lam_init)
    y_sample = _out_proj(oas.reshape(1, rows, width), x1s, sm["gate_b"], w_out_b_bf, tm=rows)

    return (y_prompt, y_sample.reshape(bs, ls, d),
            st_p[None], conv_p[None],
            k_p.reshape(bp, lp, n_heads, DIFF_VH), v_p.reshape(bp, lp, n_heads, DIFF_VH),
            st_s[None], conv_s[None],
            k_s.reshape(bs, ls, n_heads, DIFF_VH), v_s.reshape(bs, ls, n_heads, DIFF_VH))
```

```python
import functools
import math

import jax
import jax.numpy as jnp
from jax import lax
from jax.experimental import pallas as pl
from jax.experimental.pallas import tpu as pltpu

F32 = jnp.float32
BF16 = jnp.bfloat16
NORM_EPS = 1e-6

GDN_DK = 128
GDN_DV = 128
CONV_WIDTH = 4
GDN_CHUNK = 64
TRI_LEAF = 8
GDN_SEQS_PER_STEP = 2
SAMPLE_PAGES_PER_STEP = 8
SAMPLE_SLOTS = 3
DIFF_DH = 64
DIFF_VH = 2 * DIFF_DH
ROPE_DIMS = DIFF_DH // 4
ROPE_THETA = 500000.0
PAGE_SIZE = 128
MASK_VALUE = -1e30
LOG2_E = math.log2(math.e)

V7X_VMEM_BYTES = 64 * 1024 * 1024
SUBLANES = 8
LANES = 128

_NT = (((1,), (1,)), ((), ()))
_TN = (((0,), (0,)), ((), ()))


def _vmem_limit(nbytes):
    return int(min(nbytes + (8 << 20), V7X_VMEM_BYTES - (6 << 20)))


def _sigmoid(x):
    return 1.0 / (1.0 + jnp.exp2(x * -LOG2_E))


def _silu(x):
    return x * _sigmoid(x)


def _dot(a, b):
    return jnp.dot(a.astype(BF16), b.astype(BF16), preferred_element_type=F32)


def _dot_nt(a, b):
    return lax.dot_general(a.astype(BF16), b.astype(BF16), _NT, preferred_element_type=F32)


def _dot_tn(a, b):
    return lax.dot_general(a.astype(BF16), b.astype(BF16), _TN, preferred_element_type=F32)


def _rows_spec(tm, width, per_row):
    if per_row:
        return pl.BlockSpec((None, tm, width), lambda g, i: (g, i, 0))
    return pl.BlockSpec((None, 1, width), lambda g, i: (g, 0, 0))


def _const_spec(shape):
    return pl.BlockSpec(shape, lambda g, i: (0,) * len(shape), pipeline_mode=pl.Buffered(1))


def _ada_kernel(c_ref, w_ref, b_ref, o_ref):
    o_ref[...] = jnp.dot(c_ref[...], w_ref[...], preferred_element_type=F32) + b_ref[...]


def _ada(c, w, b):
    rows, d = c.shape
    n = w.shape[1]
    tn = 1024
    return pl.pallas_call(
        _ada_kernel,
        grid=(n // tn,),
        in_specs=[pl.BlockSpec((rows, d), lambda j: (0, 0)),
                  pl.BlockSpec((d, tn), lambda j: (0, j)),
                  pl.BlockSpec((1, tn), lambda j: (0, j))],
        out_specs=pl.BlockSpec((rows, tn), lambda j: (0, j)),
        out_shape=jax.ShapeDtypeStruct((rows, n), F32),
        name="ada",
    )(c, w, b)


def _gdn_in_kernel(x_ref, shift_ref, scale_ref, gain_ref, w_ref, cw_ref, alog_ref, dtb_ref, conv0_ref,
                   q_ref, k_ref, v_ref, z_ref, g_ref, beta_ref, convo_ref, xc_ref,
                   *, tm, stride, nt, qk_width, v_width, n_vheads):
    i = pl.program_id(1)
    conv_dim = 2 * qk_width + v_width
    hist = (CONV_WIDTH - 1) * stride
    hist_base = -(-hist // SUBLANES) * SUBLANES

    x = x_ref[...]
    h = x * lax.rsqrt(jnp.mean(x * x, axis=-1, keepdims=True) + NORM_EPS) * gain_ref[...]
    h = h * (1.0 + scale_ref[...]) + shift_ref[...]
    hb = h.astype(BF16)

    @pl.when(i == 0)
    def _():
        xc_ref[hist_base - hist:hist_base, :] = conv0_ref[...]

    n_qk_heads = qk_width // GDN_DK
    for c0 in range(0, conv_dim, qk_width):
        cols = slice(c0, c0 + qk_width)
        xc_ref[hist_base:hist_base + tm, cols] = jnp.dot(hb, w_ref[:, cols], preferred_element_type=F32)
        if stride % SUBLANES == 0:
            taps = [xc_ref[hist_base - hist + j * stride:hist_base - hist + j * stride + tm, cols]
                    for j in range(CONV_WIDTH)]
        else:
            win = xc_ref[0:hist_base + tm, cols]
            taps = [pltpu.roll(win, (CONV_WIDTH - 1 - j) * stride, 0)[hist_base:] for j in range(CONV_WIDTH - 1)]
            taps.append(win[hist_base:])
        y = taps[0] * cw_ref[0:1, cols]
        for j in range(1, CONV_WIDTH):
            y = y + taps[j] * cw_ref[j:j + 1, cols]
        a = _silu(y)
        if c0 < 2 * qk_width:
            out_ref = q_ref if c0 == 0 else k_ref
            post = GDN_DK ** -0.5 if c0 == 0 else 1.0
            for hd in range(n_qk_heads):
                ah = a[:, hd * GDN_DK:(hd + 1) * GDN_DK]
                nrm = lax.rsqrt(jnp.sum(ah * ah, axis=-1, keepdims=True) + NORM_EPS)
                out_ref[:, hd * GDN_DK:(hd + 1) * GDN_DK] = ah * nrm * post
        else:
            v0 = c0 - 2 * qk_width
            v_ref[:, v0:v0 + qk_width] = a

    new_hist = xc_ref[hist_base + tm - hist:hist_base + tm, :]
    xc_ref[hist_base - hist:hist_base, :] = new_hist

    @pl.when(i == nt - 1)
    def _():
        convo_ref[...] = new_hist

    for c0 in range(0, v_width, qk_width):
        zc = jnp.dot(hb, w_ref[:, conv_dim + c0:conv_dim + c0 + qk_width], preferred_element_type=F32)
        z_ref[:, c0:c0 + qk_width] = _silu(zc)

    g0 = conv_dim + v_width
    bg = jnp.dot(hb, w_ref[:, g0:g0 + 2 * n_vheads], preferred_element_type=F32)
    beta_ref[...] = _sigmoid(bg[:, :n_vheads])
    a_in = bg[:, n_vheads:] + dtb_ref[...]
    softplus = jnp.maximum(a_in, 0.0) + jnp.log1p(jnp.exp(-jnp.abs(a_in)))
    g_ref[...] = -jnp.exp(alog_ref[...]) * softplus


def _gdn_in(x, shift, scale, gain, w_in, conv_w, a_log, dt_bias, conv0, *, tm, stride):
    g_, t_, d = x.shape
    n_vheads = a_log.shape[-1]
    conv_dim = conv_w.shape[-1]
    v_width = n_vheads * GDN_DV
    qk_width = (conv_dim - v_width) // 2
    in_width = w_in.shape[1]
    nt = t_ // tm
    hist = (CONV_WIDTH - 1) * stride
    hist_base = -(-hist // SUBLANES) * SUBLANES
    per_row = shift.shape[1] != 1
    kern = functools.partial(_gdn_in_kernel, tm=tm, stride=stride, nt=nt, qk_width=qk_width,
                             v_width=v_width, n_vheads=n_vheads)
    est = (w_in.size * 2 + 2 * tm * d * 4 + 2 * tm * (2 * qk_width + 2 * v_width) * 4
           + (hist_base + tm) * conv_dim * 4 + 4 * tm * qk_width * 4 + (4 * tm * d * 4 if per_row else 0))
    outs = pl.pallas_call(
        kern,
        grid=(g_, nt),
        in_specs=[_rows_spec(tm, d, True), _rows_spec(tm, d, per_row), _rows_spec(tm, d, per_row),
                  _const_spec((1, d)), _const_spec((d, in_width)), _const_spec((CONV_WIDTH, conv_dim)),
                  _const_spec((1, n_vheads)), _const_spec((1, n_vheads)),
                  pl.BlockSpec((None, hist, conv_dim), lambda g, i: (g, 0, 0))],
        out_specs=[_rows_spec(tm, qk_width, True), _rows_spec(tm, qk_width, True),
                   _rows_spec(tm, v_width, True), _rows_spec(tm, v_width, True),
                   _rows_spec(tm, n_vheads, True), _rows_spec(tm, n_vheads, True),
                   pl.BlockSpec((None, hist, conv_dim), lambda g, i: (g, 0, 0))],
        out_shape=[jax.ShapeDtypeStruct((g_, t_, qk_width), F32), jax.ShapeDtypeStruct((g_, t_, qk_width), F32),
                   jax.ShapeDtypeStruct((g_, t_, v_width), F32), jax.ShapeDtypeStruct((g_, t_, v_width), F32),
                   jax.ShapeDtypeStruct((g_, t_, n_vheads), F32), jax.ShapeDtypeStruct((g_, t_, n_vheads), F32),
                   jax.ShapeDtypeStruct((g_, hist, conv_dim), F32)],
        scratch_shapes=[pltpu.VMEM((hist_base + tm, conv_dim), F32)],
        compiler_params=pltpu.CompilerParams(dimension_semantics=("arbitrary", "arbitrary"),
                                             vmem_limit_bytes=_vmem_limit(est)),
        name="gdn_in",
    )(x, shift, scale, gain, w_in, conv_w, a_log, dt_bias, conv0)
    return outs


def _gdn_kernel(q_ref, k_ref, v_ref, z_ref, g_ref, gt_ref, beta_ref, s0_ref, gain_ref,
                og_ref, sout_ref, s_ref, *, c, nc, nb, n_qk_heads, rep):
    n = pl.program_id(1)

    @pl.when(n == 0)
    def _():
        s_ref[...] = s0_ref[...]

    row = lax.broadcasted_iota(jnp.int32, (c, c), 0)
    col = lax.broadcasted_iota(jnp.int32, (c, c), 1)
    lower = row >= col
    strict = row > col
    eye = (row == col).astype(F32)
    leaf = min(c, TRI_LEAF)
    same_block = lambda m: (row >> int(math.log2(m))) == (col >> int(math.log2(m)))
    leaf_mask = same_block(leaf)
    merge_masks = []
    m = leaf
    while m < c:
        merge_masks.append(same_block(2 * m) & jnp.logical_not(same_block(m)))
        m *= 2

    def unit_lower_inverses(a_list):
        ds = [jnp.where(leaf_mask, a, 0.0) for a in a_list]
        ts = [eye - d for d in ds]
        ps = ds
        for _ in range(int(math.log2(leaf)) - 1):
            ps = [_dot(p, p) for p in ps]
            ts = [t + _dot(t, p) for t, p in zip(ts, ps)]
        for mask in merge_masks:
            tmp = [_dot(jnp.where(mask, a, 0.0), t) for a, t in zip(a_list, ts)]
            ts = [t - _dot(t, x) for t, x in zip(ts, tmp)]
        return ts

    seqs = range(nb)
    units = [(bi, hv) for bi in seqs for hv in range(n_qk_heads * rep)]
    qk_units = [(bi, j) for bi in seqs for j in range(n_qk_heads)]
    gc_all = [jnp.dot(lower.astype(F32), g_ref[bi], precision=lax.Precision.HIGHEST,
                      preferred_element_type=F32) for bi in seqs]
    gcr_all = [jnp.dot(gt_ref[bi], (row <= col).astype(F32), precision=lax.Precision.HIGHEST,
                       preferred_element_type=F32) for bi in seqs]
    beta_all = [beta_ref[bi] for bi in seqs]
    qs = {(bi, j): q_ref[bi, :, j * GDN_DK:(j + 1) * GDN_DK] for bi, j in qk_units}
    ks = {(bi, j): k_ref[bi, :, j * GDN_DK:(j + 1) * GDN_DK] for bi, j in qk_units}
    kks = {u: _dot_nt(ks[u], ks[u]) for u in qk_units}
    qks = {u: _dot_nt(qs[u], ks[u]) for u in qk_units}
    qk_of = lambda bi, hv: (bi, hv // rep)
    vcols = lambda hv: slice(hv * GDN_DV, (hv + 1) * GDN_DV)
    gcs = [gc_all[bi][:, hv:hv + 1] for bi, hv in units]
    betas = [beta_all[bi][:, hv:hv + 1] for bi, hv in units]
    decs = [jnp.exp(jnp.where(lower, gc - gcr_all[bi][hv:hv + 1, :], MASK_VALUE)) for (bi, hv), gc in zip(units, gcs)]
    a_list = [jnp.where(strict, kks[qk_of(*u)] * beta * dec, 0.0) for u, beta, dec in zip(units, betas, decs)]
    egs = [jnp.exp(gc) for gc in gcs]
    rhss = [jnp.concatenate([v_ref[bi, :, vcols(hv)] * beta, ks[qk_of(bi, hv)] * (beta * eg)], axis=1)
            for (bi, hv), beta, eg in zip(units, betas, egs)]
    ts = unit_lower_inverses(a_list)
    xss = [_dot(t, rhs) for t, rhs in zip(ts, rhss)]
    ss = [s_ref[bi, hv] for bi, hv in units]
    wqs = [_dot(jnp.concatenate([xs[:, GDN_DV:], qs[qk_of(*u)] * eg], axis=0), s)
           for u, xs, eg, s in zip(units, xss, egs, ss)]
    v_news = [xs[:, :GDN_DV] - wq[:c] for xs, wq in zip(xss, wqs)]
    os_ = [wq[c:] + _dot(qks[qk_of(*u)] * dec, v_new) for u, wq, dec, v_new in zip(units, wqs, decs, v_news)]
    for (bi, hv), gc, s, v_new in zip(units, gcs, ss, v_news):
        g_last = gc[c - 1:c, :]
        s_ref[bi, hv] = s * jnp.exp(g_last) + _dot_tn(ks[qk_of(bi, hv)] * jnp.exp(g_last - gc), v_new)
    for (bi, hv), o in zip(units, os_):
        on = o * lax.rsqrt(jnp.mean(o * o, axis=-1, keepdims=True) + NORM_EPS) * gain_ref[...]
        og_ref[bi, :, vcols(hv)] = (on * z_ref[bi, :, vcols(hv)]).astype(BF16)

    @pl.when(n == nc - 1)
    def _():
        sout_ref[...] = s_ref[...]


def _gdn(q, k, v, z, g, beta, s0, out_gain):
    b_, l_, qk_width = q.shape
    v_width = v.shape[-1]
    hv = g.shape[-1]
    n_qk_heads = qk_width // GDN_DK
    rep = hv // n_qk_heads
    c = min(GDN_CHUNK, l_)
    nc = l_ // c
    nb = GDN_SEQS_PER_STEP if b_ % GDN_SEQS_PER_STEP == 0 else 1
    gt = jnp.swapaxes(g.reshape(b_, nc, c, hv), 2, 3)
    kern = functools.partial(_gdn_kernel, c=c, nc=nc, nb=nb, n_qk_heads=n_qk_heads, rep=rep)
    row = lambda w: pl.BlockSpec((nb, c, w), lambda b, n: (b, n, 0))
    state = pl.BlockSpec((nb, hv, GDN_DK, GDN_DV), lambda b, n: (b, 0, 0, 0))
    og, s_out = pl.pallas_call(
        kern,
        grid=(b_ // nb, nc),
        in_specs=[row(qk_width), row(qk_width), row(v_width), row(v_width), row(hv),
                  pl.BlockSpec((nb, None, hv, c), lambda b, n: (b, n, 0, 0)), row(hv), state,
                  pl.BlockSpec((1, GDN_DV), lambda b, n: (0, 0))],
        out_specs=[row(v_width), state],
        out_shape=[jax.ShapeDtypeStruct((b_, l_, v_width), BF16),
                   jax.ShapeDtypeStruct((b_, hv, GDN_DK, GDN_DV), F32)],
        scratch_shapes=[pltpu.VMEM((nb, hv, GDN_DK, GDN_DV), F32)],
        compiler_params=pltpu.CompilerParams(dimension_semantics=("arbitrary", "arbitrary")),
        name="gdn",
    )(q, k, v, z, g, gt, beta, s0, out_gain)
    return og, s_out


def _comp_norm_rotary(xh, gain, cosf, sin_lo, sin_hi):
    lane = lax.broadcasted_iota(jnp.int32, xh.shape, 1)
    first = lane < DIFF_DH
    sq = xh * xh
    s_a = jnp.sum(jnp.where(first, sq, 0.0), axis=-1, keepdims=True)
    s_b = jnp.sum(jnp.where(first, 0.0, sq), axis=-1, keepdims=True)
    ms = jnp.where(first, s_a, s_b) * (1.0 / DIFF_DH)
    y = xh * lax.rsqrt(ms + NORM_EPS) * gain
    half = ROPE_DIMS // 2
    return y * cosf + pltpu.roll(y, LANES - half, 1) * sin_lo + pltpu.roll(y, half, 1) * sin_hi


def _mid_kernel(og_ref, x_ref, gate_ref, wo_ref,
                shkv_ref, sckv_ref, nkv_ref, wkv_ref, kg_ref,
                shb_ref, scb_ref, nb_ref, wb_ref, qg_ref,
                cos_ref, slo_ref, shi_ref,
                x1_ref, k_ref, v_ref, kb_ref, vb_ref, q_ref, z_ref, *, n_heads, v_transposed):
    x1 = x_ref[...] + gate_ref[...] * jnp.dot(og_ref[...], wo_ref[...], preferred_element_type=F32)
    x1_ref[...] = x1
    r = x1 * lax.rsqrt(jnp.mean(x1 * x1, axis=-1, keepdims=True) + NORM_EPS)
    cosf, slo, shi = cos_ref[...], slo_ref[...], shi_ref[...]
    width = n_heads * DIFF_VH

    hkv = ((r * nkv_ref[...]) * (1.0 + sckv_ref[...]) + shkv_ref[...]).astype(BF16)
    kraw = jnp.dot(hkv, wkv_ref[:, :width], preferred_element_type=F32)
    for hd in range(n_heads):
        sl = slice(hd * DIFF_VH, (hd + 1) * DIFF_VH)
        kh = _comp_norm_rotary(kraw[:, sl], kg_ref[...], cosf, slo, shi)
        k_ref[:, sl] = kh
        kb_ref[:, sl] = kh.astype(BF16)
    vraw = jnp.dot(hkv, wkv_ref[:, width:], preferred_element_type=F32)
    v_ref[...] = vraw
    if v_transposed:
        for hd in range(n_heads):
            vb_ref[hd] = jnp.transpose(vraw[:, hd * DIFF_VH:(hd + 1) * DIFF_VH]).astype(BF16)
    else:
        vb_ref[...] = vraw.astype(BF16)

    hq = ((r * nb_ref[...]) * (1.0 + scb_ref[...]) + shb_ref[...]).astype(BF16)
    qraw = jnp.dot(hq, wb_ref[:, :width], preferred_element_type=F32)
    for hd in range(n_heads):
        sl = slice(hd * DIFF_VH, (hd + 1) * DIFF_VH)
        qh = _comp_norm_rotary(qraw[:, sl], qg_ref[...], cosf, slo, shi)
        q_ref[:, sl] = (qh * (DIFF_DH ** -0.5 * LOG2_E)).astype(BF16)
    z_ref[...] = _silu(jnp.dot(hq, wb_ref[:, width:], preferred_element_type=F32))


def _mid(og, x, gate, w_out, shift_kv, scale_kv, norm_kv, w_kv, k_gain,
         shift_b, scale_b, norm_b, w_b, q_gain, cosf, sin_lo, sin_hi, *, tm, v_transposed):
    g_, t_, d = x.shape
    v_width = og.shape[-1]
    width = w_kv.shape[1] // 2
    n_heads = width // DIFF_VH
    per_row = gate.shape[1] != 1
    rope_per_group = cosf.shape[0] != 1
    rope_spec = pl.BlockSpec((None, tm, LANES), (lambda g, i: (g, i, 0)) if rope_per_group else (lambda g, i: (0, i, 0)))
    mod = _rows_spec(tm, d, per_row)
    est = ((w_out.size + w_kv.size + w_b.size) * 2 + 2 * tm * (v_width * 2 + d * 4)
           + 2 * tm * width * (4 + 4 + 4 + 2 + 2 + 2 + 4) + 8 * tm * width * 4 + (10 * tm * d * 4 if per_row else 0))
    kern = functools.partial(_mid_kernel, n_heads=n_heads, v_transposed=v_transposed)
    f32_out = jax.ShapeDtypeStruct((g_, t_, width), F32)
    bf_out = jax.ShapeDtypeStruct((g_, t_, width), BF16)
    rows_bf = _rows_spec(tm, width, True)
    if v_transposed:
        vb_out = jax.ShapeDtypeStruct((g_, n_heads, t_ // tm, DIFF_VH, tm), BF16)
        vb_spec = pl.BlockSpec((None, n_heads, None, DIFF_VH, tm), lambda g, i: (g, 0, i, 0, 0))
    else:
        vb_out, vb_spec = bf_out, rows_bf
    return pl.pallas_call(
        kern,
        grid=(g_, t_ // tm),
        in_specs=[_rows_spec(tm, v_width, True), _rows_spec(tm, d, True), mod, _const_spec((v_width, d)),
                  mod, mod, _const_spec((1, d)), _const_spec((d, 2 * width)), _const_spec((1, DIFF_VH)),
                  mod, mod, _const_spec((1, d)), _const_spec((d, 2 * width)), _const_spec((1, DIFF_VH)),
                  rope_spec, rope_spec, rope_spec],
        out_specs=[_rows_spec(tm, d, True), rows_bf, rows_bf, rows_bf, vb_spec, rows_bf, rows_bf],
        out_shape=[jax.ShapeDtypeStruct((g_, t_, d), F32), f32_out, f32_out, bf_out, vb_out, bf_out, f32_out],
        compiler_params=pltpu.CompilerParams(dimension_semantics=("arbitrary", "arbitrary"),
                                             vmem_limit_bytes=_vmem_limit(est)),
        name="mid",
    )(og, x, gate, w_out, shift_kv, scale_kv, norm_kv, w_kv, k_gain,
      shift_b, scale_b, norm_b, w_b, q_gain, cosf, sin_lo, sin_hi)


def _lambda(lam_ref, lam_init):
    lp = lam_ref[...]
    a = jnp.sum(lp[0:1] * lp[1:2], axis=-1, keepdims=True)
    b = jnp.sum(lp[2:3] * lp[3:4], axis=-1, keepdims=True)
    return jnp.exp(a) - jnp.exp(b) + lam_init


def _subln_gate(o, gain, z, lam_init):
    on = o * lax.rsqrt(jnp.mean(o * o, axis=-1, keepdims=True) + NORM_EPS) * gain
    return (on * (1.0 - lam_init)) * z


def _prompt_attn_kernel(q_ref, k_ref, vt_ref, z_ref, lam_ref, gain_ref, o_ref, acc_ref, sa_ref, sb_ref,
                        *, t, lam_init):
    i = pl.program_id(2)
    q = q_ref[...]
    lane = lax.broadcasted_iota(jnp.int32, q.shape, 1)
    zero = jnp.zeros_like(q)
    qc = [jnp.where(lane < DIFF_DH, q, zero), jnp.where(lane < DIFF_DH, zero, q)]
    acc_ref[...] = jnp.zeros_like(acc_ref)
    tv = vt_ref.shape[-1]
    nv = t // tv

    def scores(j):
        kj = k_ref[pl.ds(pl.multiple_of(j * t, t), t), :]
        return [lax.dot_general(kj, qq, _NT, preferred_element_type=F32) for qq in qc]

    def absorb(j, ss, stats):
        ms, ls = stats[:2], stats[2:]
        m_new = [jnp.maximum(m, jnp.max(sc, axis=0, keepdims=True)) for m, sc in zip(ms, ss)]
        ps = [jnp.exp2(sc - mn) for sc, mn in zip(ss, m_new)]
        alphas = [jnp.exp2(m - mn) for m, mn in zip(ms, m_new)]
        l_new = [a * l + jnp.sum(p, axis=0, keepdims=True) for a, l, p in zip(alphas, ls, ps)]
        p12 = jnp.concatenate([p.astype(BF16) for p in ps], axis=1)
        acc = jnp.concatenate(alphas, axis=1) * acc_ref[...]
        for c in range(nv):
            acc = acc + jnp.dot(vt_ref[j * nv + c], p12[c * tv:(c + 1) * tv], preferred_element_type=F32)
        acc_ref[...] = acc
        return (*m_new, *l_new)

    def put(buf, ss):
        buf[:, :t], buf[:, t:] = ss

    def get(buf):
        return [buf[:, :t], buf[:, t:]]

    neg = jnp.full((1, t), MASK_VALUE, F32)
    zer = jnp.zeros((1, t), F32)

    put(sa_ref, scores(0))

    def produce_and_absorb(cur, nxt, j, stats):
        put(nxt, scores(j + 1))
        return absorb(j, get(cur), stats)

    def body(j, stats):
        return lax.cond(j % 2 == 0,
                        lambda st: produce_and_absorb(sa_ref, sb_ref, j, st),
                        lambda st: produce_and_absorb(sb_ref, sa_ref, j, st), stats)

    stats = lax.fori_loop(0, i, body, (neg, neg, zer, zer))
    keep = lax.broadcasted_iota(jnp.int32, (t, t), 0) <= lax.broadcasted_iota(jnp.int32, (t, t), 1)
    diagonal = lambda buf: (lambda st: absorb(i, [jnp.where(keep, sc, MASK_VALUE) for sc in get(buf)], st))
    _, _, l1, l2 = lax.cond(i % 2 == 0, diagonal(sa_ref), diagonal(sb_ref), stats)
    lam = _lambda(lam_ref, lam_init)
    acc = acc_ref[...]
    o = jnp.transpose(acc[:, :t] / l1 - lam * (acc[:, t:] / l2))
    o_ref[...] = _subln_gate(o, gain_ref[...], z_ref[...], lam_init).astype(BF16)


def _prompt_attn(q, k, vt, z, lam_params, sub_gain, *, t, lam_init):
    b_, l_, width = q.shape
    n_heads = width // DIFF_VH
    tv = vt.shape[-1]
    kern = functools.partial(_prompt_attn_kernel, t=t, lam_init=lam_init)
    tile = pl.BlockSpec((None, t, DIFF_VH), lambda b, h, i: (b, i, h))
    full = pl.BlockSpec((None, l_, DIFF_VH), lambda b, h, i: (b, 0, h))
    full_t = pl.BlockSpec((None, None, l_ // tv, DIFF_VH, tv), lambda b, h, i: (b, h, 0, 0, 0))
    return pl.pallas_call(
        kern,
        grid=(b_, n_heads, l_ // t),
        in_specs=[tile, full, full_t, tile,
                  pl.BlockSpec(lam_params.shape, lambda b, h, i: (0, 0)),
                  pl.BlockSpec((1, DIFF_VH), lambda b, h, i: (0, 0))],
        out_specs=tile,
        out_shape=jax.ShapeDtypeStruct((b_, l_, width), BF16),
        scratch_shapes=[pltpu.VMEM((DIFF_VH, 2 * t), F32), pltpu.VMEM((t, 2 * t), F32),
                        pltpu.VMEM((t, 2 * t), F32)],
        compiler_params=pltpu.CompilerParams(dimension_semantics=("arbitrary", "arbitrary", "arbitrary")),
        name="prompt_attn",
    )(q, k, vt, z, lam_params, sub_gain)


def _sample_attn_kernel(pt_ref, q_ref, kc_hbm, vc_hbm, kn_ref, vn_ref, z_ref, lam_ref, gain_ref, o_ref,
                        m_ref, l_ref, acc_ref, bias_ref, kbuf, vbuf, sem,
                        *, n_seqs, n_steps, pages_per_step, n_heads, lam_init):
    step = pl.program_id(1)
    g = pl.program_id(0) * n_steps + step
    total = n_seqs * n_steps
    n_slots = kbuf.shape[0]
    ahead = n_slots - 1
    n_rows = q_ref.shape[0]
    rows = bias_ref.shape[1]
    half = n_rows // 2

    def page_copies(gs, slot):
        seq, st = gs // n_steps, gs % n_steps
        out = []
        for p in range(pages_per_step):
            page = pt_ref[seq, st * pages_per_step + p]
            dst = pl.ds(p * rows, rows)
            out.append(pltpu.make_async_copy(kc_hbm.at[page], kbuf.at[slot, dst], sem.at[slot, 0]))
            out.append(pltpu.make_async_copy(vc_hbm.at[page], vbuf.at[slot, dst], sem.at[slot, 1]))
        return out

    @pl.when(g == 0)
    def _():
        for d in range(ahead):
            for cp in page_copies(d, d):
                cp.start()

    @pl.when(g + ahead < total)
    def _():
        for cp in page_copies(g + ahead, (g + ahead) % n_slots):
            cp.start()

    def head_match(n_keys):
        r = lax.broadcasted_iota(jnp.int32, (n_rows, n_keys), 0)
        c = lax.broadcasted_iota(jnp.int32, (n_rows, n_keys), 1)
        return r, c, (r % n_heads) == (c % n_heads)

    @pl.when(step == 0)
    def _():
        m_ref[...] = jnp.full_like(m_ref, MASK_VALUE)
        l_ref[...] = jnp.zeros_like(l_ref)
        acc_ref[...] = jnp.zeros_like(acc_ref)
        _, _, same_head = head_match(rows)
        bias_ref[...] = jnp.where(same_head, 0.0, MASK_VALUE)

    def update(k_all, v_all, biases):
        s_all = lax.dot_general(q_ref[...], k_all.astype(BF16), _NT, preferred_element_type=F32)
        w = s_all.shape[1] // len(biases)
        ss = [s_all[:, n * w:(n + 1) * w] + bias for n, bias in enumerate(biases)]
        m_old = m_ref[...]
        m_new = m_old
        for sb in ss:
            m_new = jnp.maximum(m_new, jnp.max(sb, axis=1, keepdims=True))
        ps = [jnp.exp2(sb - m_new) for sb in ss]
        alpha = jnp.exp2(m_old - m_new)
        l_new = alpha * l_ref[...]
        for pb in ps:
            l_new = l_new + jnp.sum(pb, axis=1, keepdims=True)
        p_all = jnp.concatenate([pb.astype(BF16) for pb in ps], axis=1)
        acc_ref[...] = alpha * acc_ref[...] + jnp.dot(p_all, v_all.astype(BF16), preferred_element_type=F32)
        m_ref[...] = m_new
        l_ref[...] = l_new

    slot = g % n_slots
    for cp in page_copies(g, slot):
        cp.wait()
    update(kbuf[slot], vbuf[slot], [bias_ref[...]] * pages_per_step)

    @pl.when(step == n_steps - 1)
    def _():
        r, c, same_head = head_match(kn_ref.shape[0])
        causal = (c // n_heads) <= ((r % half) // n_heads)
        update(kn_ref[...], vn_ref[...], [jnp.where(same_head & causal, 0.0, MASK_VALUE)])
        lam = _lambda(lam_ref, lam_init)
        of = acc_ref[...] / l_ref[...]
        o = of[:half] - lam * of[half:]
        o_ref[...] = _subln_gate(o, gain_ref[...], z_ref[...], lam_init).astype(BF16)


def _sample_attn(page_table, qmat, cache_k, cache_v, k_new, v_new, z, lam_params, sub_gain, *, lam_init):
    b_, n_rows, _ = qmat.shape
    n_pages = page_table.shape[1]
    rows = cache_k.shape[1]
    n_heads = rows // PAGE_SIZE
    lh = z.shape[1]
    pps = SAMPLE_PAGES_PER_STEP
    n_steps = n_pages // pps
    kern = functools.partial(_sample_attn_kernel, n_seqs=b_, n_steps=n_steps, pages_per_step=pps,
                             n_heads=n_heads, lam_init=lam_init)
    per_seq = lambda r: pl.BlockSpec((None, r, DIFF_VH), lambda b, s, pt: (b, 0, 0))
    hbm = pl.BlockSpec(memory_space=pl.ANY)
    grid_spec = pltpu.PrefetchScalarGridSpec(
        num_scalar_prefetch=1,
        grid=(b_, n_steps),
        in_specs=[per_seq(n_rows), hbm, hbm,
                  per_seq(k_new.shape[1]), per_seq(v_new.shape[1]), per_seq(lh),
                  pl.BlockSpec(lam_params.shape, lambda b, s, pt: (0, 0)),
                  pl.BlockSpec((1, DIFF_VH), lambda b, s, pt: (0, 0))],
        out_specs=per_seq(lh),
        scratch_shapes=[pltpu.VMEM((n_rows, 1), F32), pltpu.VMEM((n_rows, 1), F32),
                        pltpu.VMEM((n_rows, DIFF_VH), F32), pltpu.VMEM((n_rows, rows), F32),
                        pltpu.VMEM((SAMPLE_SLOTS, pps * rows, DIFF_VH), F32),
                        pltpu.VMEM((SAMPLE_SLOTS, pps * rows, DIFF_VH), F32),
                        pltpu.SemaphoreType.DMA((SAMPLE_SLOTS, 2))],
    )
    est = SAMPLE_SLOTS * 2 * pps * rows * DIFF_VH * 4 + (3 * pps + 1) * n_rows * rows * 4
    return pl.pallas_call(
        kern,
        grid_spec=grid_spec,
        out_shape=jax.ShapeDtypeStruct((b_, lh, DIFF_VH), BF16),
        compiler_params=pltpu.CompilerParams(dimension_semantics=("arbitrary", "arbitrary"),
                                             vmem_limit_bytes=_vmem_limit(est)),
        name="sample_attn",
    )(page_table, qmat, cache_k, cache_v, k_new, v_new, z, lam_params, sub_gain)


def _out_kernel(og_ref, x_ref, gate_ref, w_ref, y_ref):
    y_ref[...] = x_ref[...] + gate_ref[...] * jnp.dot(og_ref[...], w_ref[...], preferred_element_type=F32)


def _out_proj(og, x, gate, w, *, tm):
    g_, t_, d = x.shape
    width = og.shape[-1]
    per_row = gate.shape[1] != 1
    return pl.pallas_call(
        _out_kernel,
        grid=(g_, t_ // tm),
        in_specs=[_rows_spec(tm, width, True), _rows_spec(tm, d, True), _rows_spec(tm, d, per_row),
                  _const_spec((width, d))],
        out_specs=_rows_spec(tm, d, True),
        out_shape=jax.ShapeDtypeStruct((g_, t_, d), F32),
        compiler_params=pltpu.CompilerParams(dimension_semantics=("arbitrary", "arbitrary")),
        name="out_proj",
    )(og, x, gate, w)


def _rotary_lane_tables(pos):
    half = ROPE_DIMS // 2
    inv_freq = ROPE_THETA ** (-jnp.arange(0, ROPE_DIMS, 2, dtype=F32) / ROPE_DIMS)
    ang = pos.astype(F32)[:, None] * inv_freq[None, :]
    cos, sin = jnp.cos(ang), jnp.sin(ang)
    n = pos.shape[0]
    rest = DIFF_DH - ROPE_DIMS
    comp = lambda parts: jnp.tile(jnp.concatenate(parts, axis=1), (1, DIFF_VH // DIFF_DH))
    cosf = comp([cos, cos, jnp.ones((n, rest), F32)])
    sin_lo = comp([-sin, jnp.zeros((n, half + rest), F32)])
    sin_hi = comp([jnp.zeros((n, half), F32), sin, jnp.zeros((n, rest), F32)])
    return cosf, sin_lo, sin_hi


def kernel(x_prompt, x_sample, state_gdn, state_conv, cache_k, cache_v, page_table, c_prompt, c_sample,
           ada_w_a, ada_b_a, norm_a, w_in_a, conv_w_a, a_log, dt_bias, gdn_out_gain, w_out_a,
           ada_w_kv, ada_b_kv, norm_kv, w_kv, k_gain,
           ada_w_b, ada_b_b, norm_b, w_in_b, q_gain, lam_params, subln_gain, w_out_b):
    bp, lp, d = x_prompt.shape
    bs, ls, _ = x_sample.shape
    hv = a_log.shape[-1]
    conv_dim = conv_w_a.shape[-1]
    n_heads = w_kv.shape[1] // 2 // DIFF_VH
    width = n_heads * DIFF_VH
    past = page_table.shape[1] * PAGE_SIZE
    lam_init = 0.8 - 0.6 * math.exp(-0.3 * 1)

    n_c = bp + bs
    c_all = jnp.concatenate([c_prompt, c_sample], axis=0)
    pad = -n_c % 16
    c_all = jnp.pad(c_all, ((0, pad), (0, 0))).astype(BF16)
    ada_w = jnp.concatenate([ada_w_a[0], ada_w_kv, ada_w_b[0]], axis=1).astype(BF16)
    ada_b = jnp.concatenate([ada_b_a[0], ada_b_kv, ada_b_b[0]])[None, :]
    mods = _ada(c_all, ada_w, ada_b)
    names = ("shift_a", "scale_a", "gate_a", "shift_kv", "scale_kv", "shift_b", "scale_b", "gate_b")
    mod = {nm: mods[:n_c, i * d:(i + 1) * d] for i, nm in enumerate(names)}

    w_in_bf = w_in_a[0].astype(BF16)
    w_out_a_bf = w_out_a[0].astype(BF16)
    w_kv_bf = w_kv.astype(BF16)
    w_in_b_bf = w_in_b[0].astype(BF16)
    w_out_b_bf = w_out_b[0].astype(BF16)
    norm_a2, norm_kv2, norm_b2 = norm_a[0][None, :], norm_kv[None, :], norm_b[0][None, :]
    k_gain2 = jnp.tile(k_gain, DIFF_VH // DIFF_DH)[None, :]
    q_gain2 = jnp.tile(q_gain[0], DIFF_VH // DIFF_DH)[None, :]
    out_gain2 = gdn_out_gain[0][None, :]
    sub_gain2 = subln_gain[0][None, :]
    lam_p = lam_params[0]
    a_log2, dt_bias2 = a_log[0][None, :], dt_bias[0][None, :]

    pm = {nm: v[:bp][:, None, :] for nm, v in mod.items()}
    tm = 256
    conv0 = jnp.zeros((bp, CONV_WIDTH - 1, conv_dim), F32)
    q, k, v, z, g, beta, conv_p = _gdn_in(x_prompt, pm["shift_a"], pm["scale_a"], norm_a2, w_in_bf, conv_w_a[0],
                                          a_log2, dt_bias2, conv0, tm=tm, stride=1)
    s0 = jnp.zeros((bp, hv, GDN_DK, GDN_DV), F32)
    og, st_p = _gdn(q, k, v, z, g, beta, s0, out_gain2)
    rope_p = [t[None] for t in _rotary_lane_tables(jnp.arange(lp, dtype=jnp.int32))]
    x1, k_p, v_p, kb, vb, qb, zb = _mid(og, x_prompt, pm["gate_a"], w_out_a_bf,
                                        pm["shift_kv"], pm["scale_kv"], norm_kv2, w_kv_bf, k_gain2,
                                        pm["shift_b"], pm["scale_b"], norm_b2, w_in_b_bf, q_gain2,
                                        *rope_p, tm=tm, v_transposed=True)
    oa = _prompt_attn(qb, kb, vb, zb, lam_p, sub_gain2, t=min(512, lp), lam_init=lam_init)
    y_prompt = _out_proj(oa, x1, pm["gate_b"], w_out_b_bf, tm=tm)

    rows = bs * ls
    tmaj = lambda a: jnp.swapaxes(a, 0, 1).reshape(1, a.shape[0] * a.shape[1], a.shape[2])
    smaj = lambda a: jnp.swapaxes(a.reshape(a.shape[0] // bs, bs, a.shape[1]), 0, 1)
    sm_t = {nm: jnp.tile(v[bp:], (ls, 1))[None] for nm, v in mod.items()}
    conv0_s = tmaj(state_conv[0])
    outs = _gdn_in(tmaj(x_sample), sm_t["shift_a"], sm_t["scale_a"], norm_a2, w_in_bf, conv_w_a[0],
                   a_log2, dt_bias2, conv0_s, tm=rows, stride=bs)
    q, k, v, z, g, beta = [smaj(a[0]) for a in outs[:6]]
    conv_s = smaj(outs[6][0])
    og, st_s = _gdn(q, k, v, z, g, beta, state_gdn[0], out_gain2)
    sm = {nm: jnp.repeat(v[bp:], ls, axis=0)[None] for nm, v in mod.items()}
    pos_s = past + jnp.arange(ls, dtype=jnp.int32)
    rope_s = [jnp.tile(t, (bs, 1))[None] for t in _rotary_lane_tables(pos_s)]
    xs = x_sample.reshape(1, rows, d)
    x1s, k_s, v_s, kbs, vbs, qbs, zbs = _mid(og.reshape(1, rows, -1), xs, sm["gate_a"], w_out_a_bf,
                                             sm["shift_kv"], sm["scale_kv"], norm_kv2, w_kv_bf, k_gain2,
                                             sm["shift_b"], sm["scale_b"], norm_b2, w_in_b_bf, q_gain2,
                                             *rope_s, tm=rows, v_transposed=False)
    q5 = qbs.reshape(bs, ls, n_heads, 2, DIFF_DH)
    qc = jnp.moveaxis(q5, 3, 1)
    zeros = jnp.zeros_like(qc[:, 0])
    qmat = jnp.stack([jnp.concatenate([qc[:, 0], zeros], axis=-1),
                      jnp.concatenate([zeros, qc[:, 1]], axis=-1)], axis=1)
    qmat = qmat.reshape(bs, 2 * ls * n_heads, DIFF_VH)
    tok_head = lambda a: a.reshape(bs, ls * n_heads, DIFF_VH)
    pad_keys = lambda a: jnp.pad(a, ((0, 0), (0, -a.shape[1] % LANES), (0, 0)))
    oas = _sample_attn(page_table, qmat,
                       cache_k.reshape(cache_k.shape[0], PAGE_SIZE * n_heads, DIFF_VH),
                       cache_v.reshape(cache_v.shape[0], PAGE_SIZE * n_heads, DIFF_VH),
                       pad_keys(tok_head(kbs)), pad_keys(tok_head(vbs)), tok_head(zbs), lam_p, sub_gain2,
                       lam_init=lam_init)
    y_sample = _out_proj(oas.reshape(1, rows, width), x1s, sm["gate_b"], w_out_b_bf, tm=rows)

    return (y_prompt, y_sample.reshape(bs, ls, d),
            st_p[None], conv_p[None],
            k_p.reshape(bp, lp, n_heads, DIFF_VH), v_p.reshape(bp, lp, n_heads, DIFF_VH),
            st_s[None], conv_s[None],
            k_s.reshape(bs, ls, n_heads, DIFF_VH), v_s.reshape(bs, ls, n_heads, DIFF_VH))
```

```python
import functools
import math

import jax
import jax.numpy as jnp
from jax import lax
from jax.experimental import pallas as pl
from jax.experimental.pallas import tpu as pltpu

F32 = jnp.float32
BF16 = jnp.bfloat16
NORM_EPS = 1e-6

GDN_DK = 128
GDN_DV = 128
CONV_WIDTH = 4
GDN_CHUNK = 64
TRI_LEAF = 8
GDN_SEQS_PER_STEP = 2
SAMPLE_PAGES_PER_STEP = 8
SAMPLE_SLOTS = 3
DIFF_DH = 64
DIFF_VH = 2 * DIFF_DH
ROPE_DIMS = DIFF_DH // 4
ROPE_THETA = 500000.0
PAGE_SIZE = 128
MASK_VALUE = -1e30
LOG2_E = math.log2(math.e)

V7X_VMEM_BYTES = 64 * 1024 * 1024
SUBLANES = 8
LANES = 128

_NT = (((1,), (1,)), ((), ()))
_TN = (((0,), (0,)), ((), ()))


def _vmem_limit(nbytes):
    return int(min(nbytes + (8 << 20), V7X_VMEM_BYTES - (6 << 20)))


def _sigmoid(x):
    return 1.0 / (1.0 + jnp.exp2(x * -LOG2_E))


def _silu(x):
    return x * _sigmoid(x)


def _dot(a, b):
    return jnp.dot(a.astype(BF16), b.astype(BF16), preferred_element_type=F32)


def _dot_nt(a, b):
    return lax.dot_general(a.astype(BF16), b.astype(BF16), _NT, preferred_element_type=F32)


def _dot_tn(a, b):
    return lax.dot_general(a.astype(BF16), b.astype(BF16), _TN, preferred_element_type=F32)


def _rows_spec(tm, width, per_row):
    if per_row:
        return pl.BlockSpec((None, tm, width), lambda g, i: (g, i, 0))
    return pl.BlockSpec((None, 1, width), lambda g, i: (g, 0, 0))


def _const_spec(shape):
    return pl.BlockSpec(shape, lambda g, i: (0,) * len(shape), pipeline_mode=pl.Buffered(1))


def _ada_kernel(c_ref, w_ref, b_ref, o_ref):
    o_ref[...] = jnp.dot(c_ref[...], w_ref[...], preferred_element_type=F32) + b_ref[...]


def _ada(c, w, b):
    rows, d = c.shape
    n = w.shape[1]
    tn = 1024
    return pl.pallas_call(
        _ada_kernel,
        grid=(n // tn,),
        in_specs=[pl.BlockSpec((rows, d), lambda j: (0, 0)),
                  pl.BlockSpec((d, tn), lambda j: (0, j)),
                  pl.BlockSpec((1, tn), lambda j: (0, j))],
        out_specs=pl.BlockSpec((rows, tn), lambda j: (0, j)),
        out_shape=jax.ShapeDtypeStruct((rows, n), F32),
        name="ada",
    )(c, w, b)


def _gdn_in_kernel(x_ref, shift_ref, scale_ref, gain_ref, w_ref, cw_ref, alog_ref, dtb_ref, conv0_ref,
                   q_ref, k_ref, v_ref, z_ref, g_ref, beta_ref, convo_ref, xc_ref,
                   *, tm, stride, nt, qk_width, v_width, n_vheads):
    i = pl.program_id(1)
    conv_dim = 2 * qk_width + v_width
    hist = (CONV_WIDTH - 1) * stride
    hist_base = -(-hist // SUBLANES) * SUBLANES

    x = x_ref[...]
    h = x * lax.rsqrt(jnp.mean(x * x, axis=-1, keepdims=True) + NORM_EPS) * gain_ref[...]
    h = h * (1.0 + scale_ref[...]) + shift_ref[...]
    hb = h.astype(BF16)

    @pl.when(i == 0)
    def _():
        xc_ref[hist_base - hist:hist_base, :] = conv0_ref[...]

    n_qk_heads = qk_width // GDN_DK
    for c0 in range(0, conv_dim, qk_width):
        cols = slice(c0, c0 + qk_width)
        xc_ref[hist_base:hist_base + tm, cols] = jnp.dot(hb, w_ref[:, cols], preferred_element_type=F32)
        if stride % SUBLANES == 0:
            taps = [xc_ref[hist_base - hist + j * stride:hist_base - hist + j * stride + tm, cols]
                    for j in range(CONV_WIDTH)]
        else:
            win = xc_ref[0:hist_base + tm, cols]
            taps = [pltpu.roll(win, (CONV_WIDTH - 1 - j) * stride, 0)[hist_base:] for j in range(CONV_WIDTH - 1)]
            taps.append(win[hist_base:])
        y = taps[0] * cw_ref[0:1, cols]
        for j in range(1, CONV_WIDTH):
            y = y + taps[j] * cw_ref[j:j + 1, cols]
        a = _silu(y)
        if c0 < 2 * qk_width:
            out_ref = q_ref if c0 == 0 else k_ref
            post = GDN_DK ** -0.5 if c0 == 0 else 1.0
            for hd in range(n_qk_heads):
                ah = a[:, hd * GDN_DK:(hd + 1) * GDN_DK]
                nrm = lax.rsqrt(jnp.sum(ah * ah, axis=-1, keepdims=True) + NORM_EPS)
                out_ref[:, hd * GDN_DK:(hd + 1) * GDN_DK] = ah * nrm * post
        else:
            v0 = c0 - 2 * qk_width
            v_ref[:, v0:v0 + qk_width] = a

    new_hist = xc_ref[hist_base + tm - hist:hist_base + tm, :]
    xc_ref[hist_base - hist:hist_base, :] = new_hist

    @pl.when(i == nt - 1)
    def _():
        convo_ref[...] = new_hist

    for c0 in range(0, v_width, qk_width):
        zc = jnp.dot(hb, w_ref[:, conv_dim + c0:conv_dim + c0 + qk_width], preferred_element_type=F32)
        z_ref[:, c0:c0 + qk_width] = _silu(zc)

    g0 = conv_dim + v_width
    bg = jnp.dot(hb, w_ref[:, g0:g0 + 2 * n_vheads], preferred_element_type=F32)
    beta_ref[...] = _sigmoid(bg[:, :n_vheads])
    a_in = bg[:, n_vheads:] + dtb_ref[...]
    softplus = jnp.maximum(a_in, 0.0) + jnp.log1p(jnp.exp(-jnp.abs(a_in)))
    g_ref[...] = -jnp.exp(alog_ref[...]) * softplus


def _gdn_in(x, shift, scale, gain, w_in, conv_w, a_log, dt_bias, conv0, *, tm, stride):
    g_, t_, d = x.shape
    n_vheads = a_log.shape[-1]
    conv_dim = conv_w.shape[-1]
    v_width = n_vheads * GDN_DV
    qk_width = (conv_dim - v_width) // 2
    in_width = w_in.shape[1]
    nt = t_ // tm
    hist = (CONV_WIDTH - 1) * stride
    hist_base = -(-hist // SUBLANES) * SUBLANES
    per_row = shift.shape[1] != 1
    kern = functools.partial(_gdn_in_kernel, tm=tm, stride=stride, nt=nt, qk_width=qk_width,
                             v_width=v_width, n_vheads=n_vheads)
    est = (w_in.size * 2 + 2 * tm * d * 4 + 2 * tm * (2 * qk_width + 2 * v_width) * 4
           + (hist_base + tm) * conv_dim * 4 + 4 * tm * qk_width * 4 + (4 * tm * d * 4 if per_row else 0))
    outs = pl.pallas_call(
        kern,
        grid=(g_, nt),
        in_specs=[_rows_spec(tm, d, True), _rows_spec(tm, d, per_row), _rows_spec(tm, d, per_row),
                  _const_spec((1, d)), _const_spec((d, in_width)), _const_spec((CONV_WIDTH, conv_dim)),
                  _const_spec((1, n_vheads)), _const_spec((1, n_vheads)),
                  pl.BlockSpec((None, hist, conv_dim), lambda g, i: (g, 0, 0))],
        out_specs=[_rows_spec(tm, qk_width, True), _rows_spec(tm, qk_width, True),
                   _rows_spec(tm, v_width, True), _rows_spec(tm, v_width, True),
                   _rows_spec(tm, n_vheads, True), _rows_spec(tm, n_vheads, True),
                   pl.BlockSpec((None, hist, conv_dim), lambda g, i: (g, 0, 0))],
        out_shape=[jax.ShapeDtypeStruct((g_, t_, qk_width), F32), jax.ShapeDtypeStruct((g_, t_, qk_width), F32),
                   jax.ShapeDtypeStruct((g_, t_, v_width), F32), jax.ShapeDtypeStruct((g_, t_, v_width), F32),
                   jax.ShapeDtypeStruct((g_, t_, n_vheads), F32), jax.ShapeDtypeStruct((g_, t_, n_vheads), F32),
                   jax.ShapeDtypeStruct((g_, hist, conv_dim), F32)],
        scratch_shapes=[pltpu.VMEM((hist_base + tm, conv_dim), F32)],
        compiler_params=pltpu.CompilerParams(dimension_semantics=("arbitrary", "arbitrary"),
                                             vmem_limit_bytes=_vmem_limit(est)),
        name="gdn_in",
    )(x, shift, scale, gain, w_in, conv_w, a_log, dt_bias, conv0)
    return outs


def _gdn_kernel(q_ref, k_ref, v_ref, z_ref, g_ref, gt_ref, beta_ref, s0_ref, gain_ref,
                og_ref, sout_ref, s_ref, *, c, nc, nb, n_qk_heads, rep):
    n = pl.program_id(1)

    @pl.when(n == 0)
    def _():
        s_ref[...] = s0_ref[...]

    row = lax.broadcasted_iota(jnp.int32, (c, c), 0)
    col = lax.broadcasted_iota(jnp.int32, (c, c), 1)
    lower = row >= col
    strict = row > col
    eye = (row == col).astype(F32)
    leaf = min(c, TRI_LEAF)
    same_block = lambda m: (row >> int(math.log2(m))) == (col >> int(math.log2(m)))
    leaf_mask = same_block(leaf)
    merge_masks = []
    m = leaf
    while m < c:
        merge_masks.append(same_block(2 * m) & jnp.logical_not(same_block(m)))
        m *= 2

    def unit_lower_inverses(a_list):
        ds = [jnp.where(leaf_mask, a, 0.0) for a in a_list]
        ts = [eye - d for d in ds]
        ps = ds
        for _ in range(int(math.log2(leaf)) - 1):
            ps = [_dot(p, p) for p in ps]
            ts = [t + _dot(t, p) for t, p in zip(ts, ps)]
        for mask in merge_masks:
            tmp = [_dot(jnp.where(mask, a, 0.0), t) for a, t in zip(a_list, ts)]
            ts = [t - _dot(t, x) for t, x in zip(ts, tmp)]
        return ts

    seqs = range(nb)
    units = [(bi, hv) for bi in seqs for hv in range(n_qk_heads * rep)]
    qk_units = [(bi, j) for bi in seqs for j in range(n_qk_heads)]
    gc_all = [jnp.dot(lower.astype(F32), g_ref[bi], precision=lax.Precision.HIGHEST,
                      preferred_element_type=F32) for bi in seqs]
    gcr_all = [jnp.dot(gt_ref[bi], (row <= col).astype(F32), precision=lax.Precision.HIGHEST,
                       preferred_element_type=F32) for bi in seqs]
    beta_all = [beta_ref[bi] for bi in seqs]
    qs = {(bi, j): q_ref[bi, :, j * GDN_DK:(j + 1) * GDN_DK] for bi, j in qk_units}
    ks = {(bi, j): k_ref[bi, :, j * GDN_DK:(j + 1) * GDN_DK] for bi, j in qk_units}
    kks = {u: _dot_nt(ks[u], ks[u]) for u in qk_units}
    qks = {u: _dot_nt(qs[u], ks[u]) for u in qk_units}
    qk_of = lambda bi, hv: (bi, hv // rep)
    vcols = lambda hv: slice(hv * GDN_DV, (hv + 1) * GDN_DV)
    gcs = [gc_all[bi][:, hv:hv + 1] for bi, hv in units]
    betas = [beta_all[bi][:, hv:hv + 1] for bi, hv in units]
    decs = [jnp.exp(jnp.where(lower, gc - gcr_all[bi][hv:hv + 1, :], MASK_VALUE)) for (bi, hv), gc in zip(units, gcs)]
    a_list = [jnp.where(strict, kks[qk_of(*u)] * beta * dec, 0.0) for u, beta, dec in zip(units, betas, decs)]
    egs = [jnp.exp(gc) for gc in gcs]
    rhss = [jnp.concatenate([v_ref[bi, :, vcols(hv)] * beta, ks[qk_of(bi, hv)] * (beta * eg)], axis=1)
            for (bi, hv), beta, eg in zip(units, betas, egs)]
    ts = unit_lower_inverses(a_list)
    xss = [_dot(t, rhs) for t, rhs in zip(ts, rhss)]
    ss = [s_ref[bi, hv] for bi, hv in units]
    wqs = [_dot(jnp.concatenate([xs[:, GDN_DV:], qs[qk_of(*u)] * eg], axis=0), s)
           for u, xs, eg, s in zip(units, xss, egs, ss)]
    v_news = [xs[:, :GDN_DV] - wq[:c] for xs, wq in zip(xss, wqs)]
    os_ = [wq[c:] + _dot(qks[qk_of(*u)] * dec, v_new) for u, wq, dec, v_new in zip(units, wqs, decs, v_news)]
    for (bi, hv), gc, s, v_new in zip(units, gcs, ss, v_news):
        g_last = gc[c - 1:c, :]
        s_ref[bi, hv] = s * jnp.exp(g_last) + _dot_tn(ks[qk_of(bi, hv)] * jnp.exp(g_last - gc), v_new)
    for (bi, hv), o in zip(units, os_):
        on = o * lax.rsqrt(jnp.mean(o * o, axis=-1, keepdims=True) + NORM_EPS) * gain_ref[...]
        og_ref[bi, :, vcols(hv)] = (on * z_ref[bi, :, vcols(hv)]).astype(BF16)

    @pl.when(n == nc - 1)
    def _():
        sout_ref[...] = s_ref[...]


def _gdn(q, k, v, z, g, beta, s0, out_gain):
    b_, l_, qk_width = q.shape
    v_width = v.shape[-1]
    hv = g.shape[-1]
    n_qk_heads = qk_width // GDN_DK
    rep = hv // n_qk_heads
    c = min(GDN_CHUNK, l_)
    nc = l_ // c
    nb = GDN_SEQS_PER_STEP if b_ % GDN_SEQS_PER_STEP == 0 else 1
    gt = jnp.swapaxes(g.reshape(b_, nc, c, hv), 2, 3)
    kern = functools.partial(_gdn_kernel, c=c, nc=nc, nb=nb, n_qk_heads=n_qk_heads, rep=rep)
    row = lambda w: pl.BlockSpec((nb, c, w), lambda b, n: (b, n, 0))
    state = pl.BlockSpec((nb, hv, GDN_DK, GDN_DV), lambda b, n: (b, 0, 0, 0))
    og, s_out = pl.pallas_call(
        kern,
        grid=(b_ // nb, nc),
        in_specs=[row(qk_width), row(qk_width), row(v_width), row(v_width), row(hv),
                  pl.BlockSpec((nb, None, hv, c), lambda b, n: (b, n, 0, 0)), row(hv), state,
                  pl.BlockSpec((1, GDN_DV), lambda b, n: (0, 0))],
        out_specs=[row(v_width), state],
        out_shape=[jax.ShapeDtypeStruct((b_, l_, v_width), BF16),
                   jax.ShapeDtypeStruct((b_, hv, GDN_DK, GDN_DV), F32)],
        scratch_shapes=[pltpu.VMEM((nb, hv, GDN_DK, GDN_DV), F32)],
        compiler_params=pltpu.CompilerParams(dimension_semantics=("arbitrary", "arbitrary")),
        name="gdn",
    )(q, k, v, z, g, gt, beta, s0, out_gain)
    return og, s_out


def _comp_norm_rotary(xh, gain, cosf, sin_lo, sin_hi):
    lane = lax.broadcasted_iota(jnp.int32, xh.shape, 1)
    first = lane < DIFF_DH
    sq = xh * xh
    s_a = jnp.sum(jnp.where(first, sq, 0.0), axis=-1, keepdims=True)
    s_b = jnp.sum(jnp.where(first, 0.0, sq), axis=-1, keepdims=True)
    ms = jnp.where(first, s_a, s_b) * (1.0 / DIFF_DH)
    y = xh * lax.rsqrt(ms + NORM_EPS) * gain
    half = ROPE_DIMS // 2
    return y * cosf + pltpu.roll(y, LANES - half, 1) * sin_lo + pltpu.roll(y, half, 1) * sin_hi


def _mid_kernel(og_ref, x_ref, gate_ref, wo_ref,
                shkv_ref, sckv_ref, nkv_ref, wkv_ref, kg_ref,
                shb_ref, scb_ref, nb_ref, wb_ref, qg_ref,
                cos_ref, slo_ref, shi_ref,
                x1_ref, k_ref, v_ref, kb_ref, vb_ref, q_ref, z_ref, *, n_heads, v_transposed):
    x1 = x_ref[...] + gate_ref[...] * jnp.dot(og_ref[...], wo_ref[...], preferred_element_type=F32)
    x1_ref[...] = x1
    r = x1 * lax.rsqrt(jnp.mean(x1 * x1, axis=-1, keepdims=True) + NORM_EPS)
    cosf, slo, shi = cos_ref[...], slo_ref[...], shi_ref[...]
    width = n_heads * DIFF_VH

    hkv = ((r * nkv_ref[...]) * (1.0 + sckv_ref[...]) + shkv_ref[...]).astype(BF16)
    kraw = jnp.dot(hkv, wkv_ref[:, :width], preferred_element_type=F32)
    for hd in range(n_heads):
        sl = slice(hd * DIFF_VH, (hd + 1) * DIFF_VH)
        kh = _comp_norm_rotary(kraw[:, sl], kg_ref[...], cosf, slo, shi)
        k_ref[:, sl] = kh
        kb_ref[:, sl] = kh.astype(BF16)
    vraw = jnp.dot(hkv, wkv_ref[:, width:], preferred_element_type=F32)
    v_ref[...] = vraw
    if v_transposed:
        for hd in range(n_heads):
            vb_ref[hd] = jnp.transpose(vraw[:, hd * DIFF_VH:(hd + 1) * DIFF_VH]).astype(BF16)
    else:
        vb_ref[...] = vraw.astype(BF16)

    hq = ((r * nb_ref[...]) * (1.0 + scb_ref[...]) + shb_ref[...]).astype(BF16)
    qraw = jnp.dot(hq, wb_ref[:, :width], preferred_element_type=F32)
    for hd in range(n_heads):
        sl = slice(hd * DIFF_VH, (hd + 1) * DIFF_VH)
        qh = _comp_norm_rotary(qraw[:, sl], qg_ref[...], cosf, slo, shi)
        q_ref[:, sl] = (qh * (DIFF_DH ** -0.5 * LOG2_E)).astype(BF16)
    z_ref[...] = _silu(jnp.dot(hq, wb_ref[:, width:], preferred_element_type=F32))


def _mid(og, x, gate, w_out, shift_kv, scale_kv, norm_kv, w_kv, k_gain,
         shift_b, scale_b, norm_b, w_b, q_gain, cosf, sin_lo, sin_hi, *, tm, v_transposed):
    g_, t_, d = x.shape
    v_width = og.shape[-1]
    width = w_kv.shape[1] // 2
    n_heads = width // DIFF_VH
    per_row = gate.shape[1] != 1
    rope_per_group = cosf.shape[0] != 1
    rope_spec = pl.BlockSpec((None, tm, LANES), (lambda g, i: (g, i, 0)) if rope_per_group else (lambda g, i: (0, i, 0)))
    mod = _rows_spec(tm, d, per_row)
    est = ((w_out.size + w_kv.size + w_b.size) * 2 + 2 * tm * (v_width * 2 + d * 4)
           + 2 * tm * width * (4 + 4 + 4 + 2 + 2 + 2 + 4) + 8 * tm * width * 4 + (10 * tm * d * 4 if per_row else 0))
    kern = functools.partial(_mid_kernel, n_heads=n_heads, v_transposed=v_transposed)
    f32_out = jax.ShapeDtypeStruct((g_, t_, width), F32)
    bf_out = jax.ShapeDtypeStruct((g_, t_, width), BF16)
    rows_bf = _rows_spec(tm, width, True)
    if v_transposed:
        vb_out = jax.ShapeDtypeStruct((g_, n_heads, t_ // tm, DIFF_VH, tm), BF16)
        vb_spec = pl.BlockSpec((None, n_heads, None, DIFF_VH, tm), lambda g, i: (g, 0, i, 0, 0))
    else:
        vb_out, vb_spec = bf_out, rows_bf
    return pl.pallas_call(
        kern,
        grid=(g_, t_ // tm),
        in_specs=[_rows_spec(tm, v_width, True), _rows_spec(tm, d, True), mod, _const_spec((v_width, d)),
                  mod, mod, _const_spec((1, d)), _const_spec((d, 2 * width)), _const_spec((1, DIFF_VH)),
                  mod, mod, _const_spec((1, d)), _const_spec((d, 2 * width)), _const_spec((1, DIFF_VH)),
                  rope_spec, rope_spec, rope_spec],
        out_specs=[_rows_spec(tm, d, True), rows_bf, rows_bf, rows_bf, vb_spec, rows_bf, rows_bf],
        out_shape=[jax.ShapeDtypeStruct((g_, t_, d), F32), f32_out, f32_out, bf_out, vb_out, bf_out, f32_out],
        compiler_params=pltpu.CompilerParams(dimension_semantics=("arbitrary", "arbitrary"),
                                             vmem_limit_bytes=_vmem_limit(est)),
        name="mid",
    )(og, x, gate, w_out, shift_kv, scale_kv, norm_kv, w_kv, k_gain,
      shift_b, scale_b, norm_b, w_b, q_gain, cosf, sin_lo, sin_hi)


def _lambda(lam_ref, lam_init):
    lp = lam_ref[...]
    a = jnp.sum(lp[0:1] * lp[1:2], axis=-1, keepdims=True)
    b = jnp.sum(lp[2:3] * lp[3:4], axis=-1, keepdims=True)
    return jnp.exp(a) - jnp.exp(b) + lam_init


def _subln_gate(o, gain, z, lam_init):
    on = o * lax.rsqrt(jnp.mean(o * o, axis=-1, keepdims=True) + NORM_EPS) * gain
    return (on * (1.0 - lam_init)) * z


def _prompt_attn_kernel(q_ref, k_ref, vt_ref, z_ref, lam_ref, gain_ref, o_ref, acc_ref, sa_ref, sb_ref,
                        *, t, nq, lam_init):
    i = pl.program_id(2)

    def components(tile):
        q = q_ref[pl.ds(pl.multiple_of(tile * t, t), t), :]
        lane = lax.broadcasted_iota(jnp.int32, q.shape, 1)
        zero = jnp.zeros_like(q)
        return [jnp.where(lane < DIFF_DH, q, zero), jnp.where(lane < DIFF_DH, zero, q)]

    qc = components(i)
    acc_ref[...] = jnp.zeros_like(acc_ref)
    tv = vt_ref.shape[-1]
    nv = t // tv

    def scores(j, q_parts=qc):
        kj = k_ref[pl.ds(pl.multiple_of(j * t, t), t), :]
        return [lax.dot_general(kj, qq, _NT, preferred_element_type=F32) for qq in q_parts]

    def absorb(j, ss, stats):
        ms, ls = stats[:2], stats[2:]
        m_new = [jnp.maximum(m, jnp.max(sc, axis=0, keepdims=True)) for m, sc in zip(ms, ss)]
        ps = [jnp.exp2(sc - mn) for sc, mn in zip(ss, m_new)]
        alphas = [jnp.exp2(m - mn) for m, mn in zip(ms, m_new)]
        l_new = [a * l + jnp.sum(p, axis=0, keepdims=True) for a, l, p in zip(alphas, ls, ps)]
        p12 = jnp.concatenate([p.astype(BF16) for p in ps], axis=1)
        acc = jnp.concatenate(alphas, axis=1) * acc_ref[...]
        for c in range(nv):
            acc = acc + jnp.dot(vt_ref[j * nv + c], p12[c * tv:(c + 1) * tv], preferred_element_type=F32)
        acc_ref[...] = acc
        return (*m_new, *l_new)

    def put(buf, ss):
        buf[:, :t], buf[:, t:] = ss

    def get(buf):
        return [buf[:, :t], buf[:, t:]]

    neg = jnp.full((1, t), MASK_VALUE, F32)
    zer = jnp.zeros((1, t), F32)

    @pl.when(i == 0)
    def _():
        put(sa_ref, scores(0))

    def produce_and_absorb(cur, nxt, j, stats):
        put(nxt, scores(j + 1))
        return absorb(j, get(cur), stats)

    def body(j, stats):
        return lax.cond(j % 2 == 0,
                        lambda st: produce_and_absorb(sa_ref, sb_ref, j, st),
                        lambda st: produce_and_absorb(sb_ref, sa_ref, j, st), stats)

    stats = lax.fori_loop(0, i, body, (neg, neg, zer, zer))
    keep = lax.broadcasted_iota(jnp.int32, (t, t), 0) <= lax.broadcasted_iota(jnp.int32, (t, t), 1)
    diagonal = lambda buf: (lambda st: absorb(i, [jnp.where(keep, sc, MASK_VALUE) for sc in get(buf)], st))
    _, _, l1, l2 = lax.cond(i % 2 == 0, diagonal(sa_ref), diagonal(sb_ref), stats)
    lam = _lambda(lam_ref, lam_init)
    acc = acc_ref[...]
    ot = acc[:, :t] / l1 - lam * (acc[:, t:] / l2)
    on = ot * lax.rsqrt(jnp.mean(ot * ot, axis=0, keepdims=True) + NORM_EPS) * gain_ref[...]
    o_ref[...] = ((jnp.transpose(on) * (1.0 - lam_init)) * z_ref[...]).astype(BF16)
    put(sa_ref, scores(0, components(jnp.minimum(i + 1, nq - 1))))


def _prompt_attn(q, k, vt, z, lam_params, sub_gain, *, t, lam_init):
    b_, l_, width = q.shape
    n_heads = width // DIFF_VH
    tv = vt.shape[-1]
    kern = functools.partial(_prompt_attn_kernel, t=t, nq=l_ // t, lam_init=lam_init)
    tile = pl.BlockSpec((None, t, DIFF_VH), lambda b, h, i: (b, i, h))
    full = pl.BlockSpec((None, l_, DIFF_VH), lambda b, h, i: (b, 0, h))
    full_t = pl.BlockSpec((None, None, l_ // tv, DIFF_VH, tv), lambda b, h, i: (b, h, 0, 0, 0))
    return pl.pallas_call(
        kern,
        grid=(b_, n_heads, l_ // t),
        in_specs=[full, full, full_t, tile,
                  pl.BlockSpec(lam_params.shape, lambda b, h, i: (0, 0)),
                  pl.BlockSpec((DIFF_VH, 1), lambda b, h, i: (0, 0))],
        out_specs=tile,
        out_shape=jax.ShapeDtypeStruct((b_, l_, width), BF16),
        scratch_shapes=[pltpu.VMEM((DIFF_VH, 2 * t), F32), pltpu.VMEM((t, 2 * t), F32),
                        pltpu.VMEM((t, 2 * t), F32)],
        compiler_params=pltpu.CompilerParams(dimension_semantics=("arbitrary", "arbitrary", "arbitrary")),
        name="prompt_attn",
    )(q, k, vt, z, lam_params, sub_gain)


def _sample_attn_kernel(pt_ref, q_ref, kc_hbm, vc_hbm, kn_ref, vn_ref, z_ref, lam_ref, gain_ref, o_ref,
                        m_ref, l_ref, acc_ref, bias_ref, kbuf, vbuf, sem,
                        *, n_seqs, n_steps, pages_per_step, n_heads, lam_init):
    step = pl.program_id(1)
    g = pl.program_id(0) * n_steps + step
    total = n_seqs * n_steps
    n_slots = kbuf.shape[0]
    ahead = n_slots - 1
    n_rows = q_ref.shape[0]
    rows = bias_ref.shape[1]
    half = n_rows // 2

    def page_copies(gs, slot):
        seq, st = gs // n_steps, gs % n_steps
        out = []
        for p in range(pages_per_step):
            page = pt_ref[seq, st * pages_per_step + p]
            dst = pl.ds(p * rows, rows)
            out.append(pltpu.make_async_copy(kc_hbm.at[page], kbuf.at[slot, dst], sem.at[slot, 0]))
            out.append(pltpu.make_async_copy(vc_hbm.at[page], vbuf.at[slot, dst], sem.at[slot, 1]))
        return out

    @pl.when(g == 0)
    def _():
        for d in range(ahead):
            for cp in page_copies(d, d):
                cp.start()

    @pl.when(g + ahead < total)
    def _():
        for cp in page_copies(g + ahead, (g + ahead) % n_slots):
            cp.start()

    def head_match(n_keys):
        r = lax.broadcasted_iota(jnp.int32, (n_rows, n_keys), 0)
        c = lax.broadcasted_iota(jnp.int32, (n_rows, n_keys), 1)
        return r, c, (r % n_heads) == (c % n_heads)

    @pl.when(step == 0)
    def _():
        m_ref[...] = jnp.full_like(m_ref, MASK_VALUE)
        l_ref[...] = jnp.zeros_like(l_ref)
        acc_ref[...] = jnp.zeros_like(acc_ref)
        _, _, same_head = head_match(rows)
        bias_ref[...] = jnp.where(same_head, 0.0, MASK_VALUE)

    def update(k_all, v_all, biases):
        s_all = lax.dot_general(q_ref[...], k_all.astype(BF16), _NT, preferred_element_type=F32)
        w = s_all.shape[1] // len(biases)
        ss = [s_all[:, n * w:(n + 1) * w] + bias for n, bias in enumerate(biases)]
        m_old = m_ref[...]
        m_new = m_old
        for sb in ss:
            m_new = jnp.maximum(m_new, jnp.max(sb, axis=1, keepdims=True))
        ps = [jnp.exp2(sb - m_new) for sb in ss]
        alpha = jnp.exp2(m_old - m_new)
        l_new = alpha * l_ref[...]
        for pb in ps:
            l_new = l_new + jnp.sum(pb, axis=1, keepdims=True)
        p_all = jnp.concatenate([pb.astype(BF16) for pb in ps], axis=1)
        acc_ref[...] = alpha * acc_ref[...] + jnp.dot(p_all, v_all.astype(BF16), preferred_element_type=F32)
        m_ref[...] = m_new
        l_ref[...] = l_new

    slot = g % n_slots
    for cp in page_copies(g, slot):
        cp.wait()
    update(kbuf[slot], vbuf[slot], [bias_ref[...]] * pages_per_step)

    @pl.when(step == n_steps - 1)
    def _():
        r, c, same_head = head_match(kn_ref.shape[0])
        causal = (c // n_heads) <= ((r % half) // n_heads)
        update(kn_ref[...], vn_ref[...], [jnp.where(same_head & causal, 0.0, MASK_VALUE)])
        lam = _lambda(lam_ref, lam_init)
        of = acc_ref[...] / l_ref[...]
        o = of[:half] - lam * of[half:]
        o_ref[...] = _subln_gate(o, gain_ref[...], z_ref[...], lam_init).astype(BF16)


def _sample_attn(page_table, qmat, cache_k, cache_v, k_new, v_new, z, lam_params, sub_gain, *, lam_init):
    b_, n_rows, _ = qmat.shape
    n_pages = page_table.shape[1]
    rows = cache_k.shape[1]
    n_heads = rows // PAGE_SIZE
    lh = z.shape[1]
    pps = SAMPLE_PAGES_PER_STEP
    n_steps = n_pages // pps
    kern = functools.partial(_sample_attn_kernel, n_seqs=b_, n_steps=n_steps, pages_per_step=pps,
                             n_heads=n_heads, lam_init=lam_init)
    per_seq = lambda r: pl.BlockSpec((None, r, DIFF_VH), lambda b, s, pt: (b, 0, 0))
    hbm = pl.BlockSpec(memory_space=pl.ANY)
    grid_spec = pltpu.PrefetchScalarGridSpec(
        num_scalar_prefetch=1,
        grid=(b_, n_steps),
        in_specs=[per_seq(n_rows), hbm, hbm,
                  per_seq(k_new.shape[1]), per_seq(v_new.shape[1]), per_seq(lh),
                  pl.BlockSpec(lam_params.shape, lambda b, s, pt: (0, 0)),
                  pl.BlockSpec((1, DIFF_VH), lambda b, s, pt: (0, 0))],
        out_specs=per_seq(lh),
        scratch_shapes=[pltpu.VMEM((n_rows, 1), F32), pltpu.VMEM((n_rows, 1), F32),
                        pltpu.VMEM((n_rows, DIFF_VH), F32), pltpu.VMEM((n_rows, rows), F32),
                        pltpu.VMEM((SAMPLE_SLOTS, pps * rows, DIFF_VH), F32),
                        pltpu.VMEM((SAMPLE_SLOTS, pps * rows, DIFF_VH), F32),
                        pltpu.SemaphoreType.DMA((SAMPLE_SLOTS, 2))],
    )
    est = SAMPLE_SLOTS * 2 * pps * rows * DIFF_VH * 4 + (3 * pps + 1) * n_rows * rows * 4
    return pl.pallas_call(
        kern,
        grid_spec=grid_spec,
        out_shape=jax.ShapeDtypeStruct((b_, lh, DIFF_VH), BF16),
        compiler_params=pltpu.CompilerParams(dimension_semantics=("arbitrary", "arbitrary"),
                                             vmem_limit_bytes=_vmem_limit(est)),
        name="sample_attn",
    )(page_table, qmat, cache_k, cache_v, k_new, v_new, z, lam_params, sub_gain)


def _out_kernel(og_ref, x_ref, gate_ref, w_ref, y_ref):
    y_ref[...] = x_ref[...] + gate_ref[...] * jnp.dot(og_ref[...], w_ref[...], preferred_element_type=F32)


def _out_proj(og, x, gate, w, *, tm):
    g_, t_, d = x.shape
    width = og.shape[-1]
    per_row = gate.shape[1] != 1
    return pl.pallas_call(
        _out_kernel,
        grid=(g_, t_ // tm),
        in_specs=[_rows_spec(tm, width, True), _rows_spec(tm, d, True), _rows_spec(tm, d, per_row),
                  _const_spec((width, d))],
        out_specs=_rows_spec(tm, d, True),
        out_shape=jax.ShapeDtypeStruct((g_, t_, d), F32),
        compiler_params=pltpu.CompilerParams(dimension_semantics=("arbitrary", "arbitrary")),
        name="out_proj",
    )(og, x, gate, w)


def _rotary_lane_tables(pos):
    half = ROPE_DIMS // 2
    inv_freq = ROPE_THETA ** (-jnp.arange(0, ROPE_DIMS, 2, dtype=F32) / ROPE_DIMS)
    ang = pos.astype(F32)[:, None] * inv_freq[None, :]
    cos, sin = jnp.cos(ang), jnp.sin(ang)
    n = pos.shape[0]
    rest = DIFF_DH - ROPE_DIMS
    comp = lambda parts: jnp.tile(jnp.concatenate(parts, axis=1), (1, DIFF_VH // DIFF_DH))
    cosf = comp([cos, cos, jnp.ones((n, rest), F32)])
    sin_lo = comp([-sin, jnp.zeros((n, half + rest), F32)])
    sin_hi = comp([jnp.zeros((n, half), F32), sin, jnp.zeros((n, rest), F32)])
    return cosf, sin_lo, sin_hi


def kernel(x_prompt, x_sample, state_gdn, state_conv, cache_k, cache_v, page_table, c_prompt, c_sample,
           ada_w_a, ada_b_a, norm_a, w_in_a, conv_w_a, a_log, dt_bias, gdn_out_gain, w_out_a,
           ada_w_kv, ada_b_kv, norm_kv, w_kv, k_gain,
           ada_w_b, ada_b_b, norm_b, w_in_b, q_gain, lam_params, subln_gain, w_out_b):
    bp, lp, d = x_prompt.shape
    bs, ls, _ = x_sample.shape
    hv = a_log.shape[-1]
    conv_dim = conv_w_a.shape[-1]
    n_heads = w_kv.shape[1] // 2 // DIFF_VH
    width = n_heads * DIFF_VH
    past = page_table.shape[1] * PAGE_SIZE
    lam_init = 0.8 - 0.6 * math.exp(-0.3 * 1)

    n_c = bp + bs
    c_all = jnp.concatenate([c_prompt, c_sample], axis=0)
    pad = -n_c % 16
    c_all = jnp.pad(c_all, ((0, pad), (0, 0))).astype(BF16)
    ada_w = jnp.concatenate([ada_w_a[0], ada_w_kv, ada_w_b[0]], axis=1).astype(BF16)
    ada_b = jnp.concatenate([ada_b_a[0], ada_b_kv, ada_b_b[0]])[None, :]
    mods = _ada(c_all, ada_w, ada_b)
    names = ("shift_a", "scale_a", "gate_a", "shift_kv", "scale_kv", "shift_b", "scale_b", "gate_b")
    mod = {nm: mods[:n_c, i * d:(i + 1) * d] for i, nm in enumerate(names)}

    w_in_bf = w_in_a[0].astype(BF16)
    w_out_a_bf = w_out_a[0].astype(BF16)
    w_kv_bf = w_kv.astype(BF16)
    w_in_b_bf = w_in_b[0].astype(BF16)
    w_out_b_bf = w_out_b[0].astype(BF16)
    norm_a2, norm_kv2, norm_b2 = norm_a[0][None, :], norm_kv[None, :], norm_b[0][None, :]
    k_gain2 = jnp.tile(k_gain, DIFF_VH // DIFF_DH)[None, :]
    q_gain2 = jnp.tile(q_gain[0], DIFF_VH // DIFF_DH)[None, :]
    out_gain2 = gdn_out_gain[0][None, :]
    sub_gain2 = subln_gain[0][None, :]
    lam_p = lam_params[0]
    a_log2, dt_bias2 = a_log[0][None, :], dt_bias[0][None, :]

    pm = {nm: v[:bp][:, None, :] for nm, v in mod.items()}
    tm = 256
    conv0 = jnp.zeros((bp, CONV_WIDTH - 1, conv_dim), F32)
    q, k, v, z, g, beta, conv_p = _gdn_in(x_prompt, pm["shift_a"], pm["scale_a"], norm_a2, w_in_bf, conv_w_a[0],
                                          a_log2, dt_bias2, conv0, tm=tm, stride=1)
    s0 = jnp.zeros((bp, hv, GDN_DK, GDN_DV), F32)
    og, st_p = _gdn(q, k, v, z, g, beta, s0, out_gain2)
    rope_p = [t[None] for t in _rotary_lane_tables(jnp.arange(lp, dtype=jnp.int32))]
    x1, k_p, v_p, kb, vb, qb, zb = _mid(og, x_prompt, pm["gate_a"], w_out_a_bf,
                                        pm["shift_kv"], pm["scale_kv"], norm_kv2, w_kv_bf, k_gain2,
                                        pm["shift_b"], pm["scale_b"], norm_b2, w_in_b_bf, q_gain2,
                                        *rope_p, tm=tm, v_transposed=True)
    oa = _prompt_attn(qb, kb, vb, zb, lam_p, subln_gain[0][:, None], t=min(512, lp), lam_init=lam_init)
    y_prompt = _out_proj(oa, x1, pm["gate_b"], w_out_b_bf, tm=tm)

    rows = bs * ls
    tmaj = lambda a: jnp.swapaxes(a, 0, 1).reshape(1, a.shape[0] * a.shape[1], a.shape[2])
    smaj = lambda a: jnp.swapaxes(a.reshape(a.shape[0] // bs, bs, a.shape[1]), 0, 1)
    sm_t = {nm: jnp.tile(v[bp:], (ls, 1))[None] for nm, v in mod.items()}
    conv0_s = tmaj(state_conv[0])
    outs = _gdn_in(tmaj(x_sample), sm_t["shift_a"], sm_t["scale_a"], norm_a2, w_in_bf, conv_w_a[0],
                   a_log2, dt_bias2, conv0_s, tm=rows, stride=bs)
    q, k, v, z, g, beta = [smaj(a[0]) for a in outs[:6]]
    conv_s = smaj(outs[6][0])
    og, st_s = _gdn(q, k, v, z, g, beta, state_gdn[0], out_gain2)
    sm = {nm: jnp.repeat(v[bp:], ls, axis=0)[None] for nm, v in mod.items()}
    pos_s = past + jnp.arange(ls, dtype=jnp.int32)
    rope_s = [jnp.tile(t, (bs, 1))[None] for t in _rotary_lane_tables(pos_s)]
    xs = x_sample.reshape(1, rows, d)
    x1s, k_s, v_s, kbs, vbs, qbs, zbs = _mid(og.reshape(1, rows, -1), xs, sm["gate_a"], w_out_a_bf,
                                             sm["shift_kv"], sm["scale_kv"], norm_kv2, w_kv_bf, k_gain2,
                                             sm["shift_b"], sm["scale_b"], norm_b2, w_in_b_bf, q_gain2,
                                             *rope_s, tm=rows, v_transposed=False)
    q5 = qbs.reshape(bs, ls, n_heads, 2, DIFF_DH)
    qc = jnp.moveaxis(q5, 3, 1)
    zeros = jnp.zeros_like(qc[:, 0])
    qmat = jnp.stack([jnp.concatenate([qc[:, 0], zeros], axis=-1),
                      jnp.concatenate([zeros, qc[:, 1]], axis=-1)], axis=1)
    qmat = qmat.reshape(bs, 2 * ls * n_heads, DIFF_VH)
    tok_head = lambda a: a.reshape(bs, ls * n_heads, DIFF_VH)
    pad_keys = lambda a: jnp.pad(a, ((0, 0), (0, -a.shape[1] % LANES), (0, 0)))
    oas = _sample_attn(page_table, qmat,
                       cache_k.reshape(cache_k.shape[0], PAGE_SIZE * n_heads, DIFF_VH),
                       cache_v.reshape(cache_v.shape[0], PAGE_SIZE * n_heads, DIFF_VH),
                       pad_keys(tok_head(kbs)), pad_keys(tok_head(vbs)), tok_head(zbs), lam_p, sub_gain2,
                       lam_init=lam_init)
    y_sample = _out_proj(oas.reshape(1, rows, width), x1s, sm["gate_b"], w_out_b_bf, tm=rows)

    return (y_prompt, y_sample.reshape(bs, ls, d),
            st_p[None], conv_p[None],
            k_p.reshape(bp, lp, n_heads, DIFF_VH), v_p.reshape(bp, lp, n_heads, DIFF_VH),
            st_s[None], conv_s[None],
            k_s.reshape(bs, ls, n_heads, DIFF_VH), v_s.reshape(bs, ls, n_heads, DIFF_VH))
```

```python
import functools
import math

import jax
import jax.numpy as jnp
from jax import lax
from jax.experimental import pallas as pl
from jax.experimental.pallas import tpu as pltpu

F32 = jnp.float32
BF16 = jnp.bfloat16
NORM_EPS = 1e-6

GDN_DK = 128
GDN_DV = 128
CONV_WIDTH = 4
GDN_CHUNK = 64
TRI_LEAF = 8
GDN_SEQS_PER_STEP = 2
SAMPLE_PAGES_PER_STEP = 8
SAMPLE_SLOTS = 3
ROW_TILE = 256
OUT_ROW_TILE = 512
ATTN_TILE = 512
DIFF_DH = 64
DIFF_VH = 2 * DIFF_DH
ROPE_DIMS = DIFF_DH // 4
ROPE_THETA = 500000.0
PAGE_SIZE = 128
MASK_VALUE = -1e30
LOG2_E = math.log2(math.e)

V7X_VMEM_BYTES = 64 * 1024 * 1024
SUBLANES = 8
LANES = 128

_NT = (((1,), (1,)), ((), ()))
_TN = (((0,), (0,)), ((), ()))


def _vmem_limit(nbytes):
    return int(min(nbytes + (8 << 20), V7X_VMEM_BYTES - (6 << 20)))


def _sigmoid(x):
    return 1.0 / (1.0 + jnp.exp2(x * -LOG2_E))


def _silu(x):
    return x * _sigmoid(x)


def _dot(a, b):
    return jnp.dot(a.astype(BF16), b.astype(BF16), preferred_element_type=F32)


def _dot_nt(a, b):
    return lax.dot_general(a.astype(BF16), b.astype(BF16), _NT, preferred_element_type=F32)


def _dot_tn(a, b):
    return lax.dot_general(a.astype(BF16), b.astype(BF16), _TN, preferred_element_type=F32)


def _rows_spec(tm, width, per_row):
    if per_row:
        return pl.BlockSpec((None, tm, width), lambda g, i: (g, i, 0))
    return pl.BlockSpec((None, 1, width), lambda g, i: (g, 0, 0))


def _const_spec(shape):
    return pl.BlockSpec(shape, lambda g, i: (0,) * len(shape), pipeline_mode=pl.Buffered(1))


def _ada_kernel(c_ref, w_ref, b_ref, o_ref):
    o_ref[...] = jnp.dot(c_ref[...], w_ref[...], preferred_element_type=F32) + b_ref[...]


def _ada(c, w, b):
    rows, d = c.shape
    n = w.shape[1]
    tn = 1024
    return pl.pallas_call(
        _ada_kernel,
        grid=(n // tn,),
        in_specs=[pl.BlockSpec((rows, d), lambda j: (0, 0)),
                  pl.BlockSpec((d, tn), lambda j: (0, j)),
                  pl.BlockSpec((1, tn), lambda j: (0, j))],
        out_specs=pl.BlockSpec((rows, tn), lambda j: (0, j)),
        out_shape=jax.ShapeDtypeStruct((rows, n), F32),
        name="ada",
    )(c, w, b)


def _gdn_in_kernel(x_ref, shift_ref, scale_ref, gain_ref, w_ref, cw_ref, alog_ref, dtb_ref, conv0_ref,
                   q_ref, k_ref, v_ref, z_ref, g_ref, beta_ref, convo_ref, xc_ref,
                   *, tm, stride, nt, qk_width, v_width, n_vheads):
    i = pl.program_id(1)
    conv_dim = 2 * qk_width + v_width
    hist = (CONV_WIDTH - 1) * stride
    hist_base = -(-hist // SUBLANES) * SUBLANES

    x = x_ref[...]
    h = x * lax.rsqrt(jnp.mean(x * x, axis=-1, keepdims=True) + NORM_EPS) * gain_ref[...]
    h = h * (1.0 + scale_ref[...]) + shift_ref[...]
    hb = h.astype(BF16)

    @pl.when(i == 0)
    def _():
        xc_ref[hist_base - hist:hist_base, :] = conv0_ref[...]

    n_qk_heads = qk_width // GDN_DK
    for c0 in range(0, conv_dim, qk_width):
        cols = slice(c0, c0 + qk_width)
        xc_ref[hist_base:hist_base + tm, cols] = jnp.dot(hb, w_ref[:, cols], preferred_element_type=F32)
        if stride % SUBLANES == 0:
            taps = [xc_ref[hist_base - hist + j * stride:hist_base - hist + j * stride + tm, cols]
                    for j in range(CONV_WIDTH)]
        else:
            win = xc_ref[0:hist_base + tm, cols]
            taps = [pltpu.roll(win, (CONV_WIDTH - 1 - j) * stride, 0)[hist_base:] for j in range(CONV_WIDTH - 1)]
            taps.append(win[hist_base:])
        y = taps[0] * cw_ref[0:1, cols]
        for j in range(1, CONV_WIDTH):
            y = y + taps[j] * cw_ref[j:j + 1, cols]
        a = _silu(y)
        if c0 < 2 * qk_width:
            out_ref = q_ref if c0 == 0 else k_ref
            post = GDN_DK ** -0.5 if c0 == 0 else 1.0
            for hd in range(n_qk_heads):
                ah = a[:, hd * GDN_DK:(hd + 1) * GDN_DK]
                nrm = lax.rsqrt(jnp.sum(ah * ah, axis=-1, keepdims=True) + NORM_EPS)
                out_ref[:, hd * GDN_DK:(hd + 1) * GDN_DK] = ah * nrm * post
        else:
            v0 = c0 - 2 * qk_width
            v_ref[:, v0:v0 + qk_width] = a

    new_hist = xc_ref[hist_base + tm - hist:hist_base + tm, :]
    xc_ref[hist_base - hist:hist_base, :] = new_hist

    @pl.when(i == nt - 1)
    def _():
        convo_ref[...] = new_hist

    for c0 in range(0, v_width, qk_width):
        zc = jnp.dot(hb, w_ref[:, conv_dim + c0:conv_dim + c0 + qk_width], preferred_element_type=F32)
        z_ref[:, c0:c0 + qk_width] = _silu(zc)

    g0 = conv_dim + v_width
    bg = jnp.dot(hb, w_ref[:, g0:g0 + 2 * n_vheads], preferred_element_type=F32)
    beta_ref[...] = _sigmoid(bg[:, :n_vheads])
    a_in = bg[:, n_vheads:] + dtb_ref[...]
    softplus = jnp.maximum(a_in, 0.0) + jnp.log1p(jnp.exp(-jnp.abs(a_in)))
    g_ref[...] = -jnp.exp(alog_ref[...]) * softplus


def _gdn_in(x, shift, scale, gain, w_in, conv_w, a_log, dt_bias, conv0, *, tm, stride):
    g_, t_, d = x.shape
    n_vheads = a_log.shape[-1]
    conv_dim = conv_w.shape[-1]
    v_width = n_vheads * GDN_DV
    qk_width = (conv_dim - v_width) // 2
    in_width = w_in.shape[1]
    nt = t_ // tm
    hist = (CONV_WIDTH - 1) * stride
    hist_base = -(-hist // SUBLANES) * SUBLANES
    per_row = shift.shape[1] != 1
    kern = functools.partial(_gdn_in_kernel, tm=tm, stride=stride, nt=nt, qk_width=qk_width,
                             v_width=v_width, n_vheads=n_vheads)
    est = (w_in.size * 2 + 2 * tm * d * 4 + 2 * tm * (2 * qk_width + 2 * v_width) * 4
           + (hist_base + tm) * conv_dim * 4 + 4 * tm * qk_width * 4 + (4 * tm * d * 4 if per_row else 0))
    outs = pl.pallas_call(
        kern,
        grid=(g_, nt),
        in_specs=[_rows_spec(tm, d, True), _rows_spec(tm, d, per_row), _rows_spec(tm, d, per_row),
                  _const_spec((1, d)), _const_spec((d, in_width)), _const_spec((CONV_WIDTH, conv_dim)),
                  _const_spec((1, n_vheads)), _const_spec((1, n_vheads)),
                  pl.BlockSpec((None, hist, conv_dim), lambda g, i: (g, 0, 0))],
        out_specs=[_rows_spec(tm, qk_width, True), _rows_spec(tm, qk_width, True),
                   _rows_spec(tm, v_width, True), _rows_spec(tm, v_width, True),
                   _rows_spec(tm, n_vheads, True), _rows_spec(tm, n_vheads, True),
                   pl.BlockSpec((None, hist, conv_dim), lambda g, i: (g, 0, 0))],
        out_shape=[jax.ShapeDtypeStruct((g_, t_, qk_width), F32), jax.ShapeDtypeStruct((g_, t_, qk_width), F32),
                   jax.ShapeDtypeStruct((g_, t_, v_width), F32), jax.ShapeDtypeStruct((g_, t_, v_width), F32),
                   jax.ShapeDtypeStruct((g_, t_, n_vheads), F32), jax.ShapeDtypeStruct((g_, t_, n_vheads), F32),
                   jax.ShapeDtypeStruct((g_, hist, conv_dim), F32)],
        scratch_shapes=[pltpu.VMEM((hist_base + tm, conv_dim), F32)],
        compiler_params=pltpu.CompilerParams(dimension_semantics=("arbitrary", "arbitrary"),
                                             vmem_limit_bytes=_vmem_limit(est)),
        name="gdn_in",
    )(x, shift, scale, gain, w_in, conv_w, a_log, dt_bias, conv0)
    return outs


def _gdn_kernel(q_ref, k_ref, v_ref, z_ref, g_ref, gt_ref, beta_ref, s0_ref, gain_ref,
                og_ref, sout_ref, s_ref, *, c, nc, nb, n_qk_heads, rep):
    n = pl.program_id(1)

    @pl.when(n == 0)
    def _():
        s_ref[...] = s0_ref[...]

    row = lax.broadcasted_iota(jnp.int32, (c, c), 0)
    col = lax.broadcasted_iota(jnp.int32, (c, c), 1)
    lower = row >= col
    strict = row > col
    eye = (row == col).astype(F32)
    leaf = min(c, TRI_LEAF)
    same_block = lambda m: (row >> int(math.log2(m))) == (col >> int(math.log2(m)))
    leaf_mask = same_block(leaf)
    merge_masks = []
    m = leaf
    while m < c:
        merge_masks.append(same_block(2 * m) & jnp.logical_not(same_block(m)))
        m *= 2

    def unit_lower_inverses(a_list):
        ds = [jnp.where(leaf_mask, a, 0.0) for a in a_list]
        ts = [eye - d for d in ds]
        ps = ds
        for _ in range(int(math.log2(leaf)) - 1):
            ps = [_dot(p, p) for p in ps]
            ts = [t + _dot(t, p) for t, p in zip(ts, ps)]
        for mask in merge_masks:
            tmp = [_dot(jnp.where(mask, a, 0.0), t) for a, t in zip(a_list, ts)]
            ts = [t - _dot(t, x) for t, x in zip(ts, tmp)]
        return ts

    seqs = range(nb)
    units = [(bi, hv) for bi in seqs for hv in range(n_qk_heads * rep)]
    qk_units = [(bi, j) for bi in seqs for j in range(n_qk_heads)]
    gc_all = [jnp.dot(lower.astype(F32), g_ref[bi], precision=lax.Precision.HIGHEST,
                      preferred_element_type=F32) for bi in seqs]
    gcr_all = [jnp.dot(gt_ref[bi], (row <= col).astype(F32), precision=lax.Precision.HIGHEST,
                       preferred_element_type=F32) for bi in seqs]
    beta_all = [beta_ref[bi] for bi in seqs]
    qs = {(bi, j): q_ref[bi, :, j * GDN_DK:(j + 1) * GDN_DK] for bi, j in qk_units}
    ks = {(bi, j): k_ref[bi, :, j * GDN_DK:(j + 1) * GDN_DK] for bi, j in qk_units}
    kks = {u: _dot_nt(ks[u], ks[u]) for u in qk_units}
    qks = {u: _dot_nt(qs[u], ks[u]) for u in qk_units}
    qk_of = lambda bi, hv: (bi, hv // rep)
    vcols = lambda hv: slice(hv * GDN_DV, (hv + 1) * GDN_DV)
    gcs = [gc_all[bi][:, hv:hv + 1] for bi, hv in units]
    betas = [beta_all[bi][:, hv:hv + 1] for bi, hv in units]
    decs = [jnp.exp(jnp.where(lower, gc - gcr_all[bi][hv:hv + 1, :], MASK_VALUE)) for (bi, hv), gc in zip(units, gcs)]
    a_list = [jnp.where(strict, kks[qk_of(*u)] * beta * dec, 0.0) for u, beta, dec in zip(units, betas, decs)]
    egs = [jnp.exp(gc) for gc in gcs]
    rhss = [jnp.concatenate([v_ref[bi, :, vcols(hv)] * beta, ks[qk_of(bi, hv)] * (beta * eg)], axis=1)
            for (bi, hv), beta, eg in zip(units, betas, egs)]
    ts = unit_lower_inverses(a_list)
    xss = [_dot(t, rhs) for t, rhs in zip(ts, rhss)]
    ss = [s_ref[bi, hv] for bi, hv in units]
    wqs = [_dot(jnp.concatenate([xs[:, GDN_DV:], qs[qk_of(*u)] * eg], axis=0), s)
           for u, xs, eg, s in zip(units, xss, egs, ss)]
    v_news = [xs[:, :GDN_DV] - wq[:c] for xs, wq in zip(xss, wqs)]
    os_ = [wq[c:] + _dot(qks[qk_of(*u)] * dec, v_new) for u, wq, dec, v_new in zip(units, wqs, decs, v_news)]
    for (bi, hv), gc, s, v_new in zip(units, gcs, ss, v_news):
        g_last = gc[c - 1:c, :]
        s_ref[bi, hv] = s * jnp.exp(g_last) + _dot_tn(ks[qk_of(bi, hv)] * jnp.exp(g_last - gc), v_new)
    for (bi, hv), o in zip(units, os_):
        on = o * lax.rsqrt(jnp.mean(o * o, axis=-1, keepdims=True) + NORM_EPS) * gain_ref[...]
        og_ref[bi, :, vcols(hv)] = (on * z_ref[bi, :, vcols(hv)]).astype(BF16)

    @pl.when(n == nc - 1)
    def _():
        sout_ref[...] = s_ref[...]


def _gdn(q, k, v, z, g, beta, s0, out_gain):
    b_, l_, qk_width = q.shape
    v_width = v.shape[-1]
    hv = g.shape[-1]
    n_qk_heads = qk_width // GDN_DK
    rep = hv // n_qk_heads
    c = min(GDN_CHUNK, l_)
    nc = l_ // c
    nb = GDN_SEQS_PER_STEP * (1 if c == GDN_CHUNK else 2)
    nb = nb if b_ % nb == 0 else 1
    gt = jnp.swapaxes(g.reshape(b_, nc, c, hv), 2, 3)
    kern = functools.partial(_gdn_kernel, c=c, nc=nc, nb=nb, n_qk_heads=n_qk_heads, rep=rep)
    row = lambda w: pl.BlockSpec((nb, c, w), lambda b, n: (b, n, 0))
    state = pl.BlockSpec((nb, hv, GDN_DK, GDN_DV), lambda b, n: (b, 0, 0, 0))
    og, s_out = pl.pallas_call(
        kern,
        grid=(b_ // nb, nc),
        in_specs=[row(qk_width), row(qk_width), row(v_width), row(v_width), row(hv),
                  pl.BlockSpec((nb, None, hv, c), lambda b, n: (b, n, 0, 0)), row(hv), state,
                  pl.BlockSpec((1, GDN_DV), lambda b, n: (0, 0))],
        out_specs=[row(v_width), state],
        out_shape=[jax.ShapeDtypeStruct((b_, l_, v_width), BF16),
                   jax.ShapeDtypeStruct((b_, hv, GDN_DK, GDN_DV), F32)],
        scratch_shapes=[pltpu.VMEM((nb, hv, GDN_DK, GDN_DV), F32)],
        compiler_params=pltpu.CompilerParams(dimension_semantics=("arbitrary", "arbitrary")),
        name="gdn",
    )(q, k, v, z, g, gt, beta, s0, out_gain)
    return og, s_out


def _comp_norm_rotary(xh, gain, cosf, sin_lo, sin_hi):
    lane = lax.broadcasted_iota(jnp.int32, xh.shape, 1)
    first = lane < DIFF_DH
    sq = xh * xh
    s_a = jnp.sum(jnp.where(first, sq, 0.0), axis=-1, keepdims=True)
    s_b = jnp.sum(jnp.where(first, 0.0, sq), axis=-1, keepdims=True)
    ms = jnp.where(first, s_a, s_b) * (1.0 / DIFF_DH)
    y = xh * lax.rsqrt(ms + NORM_EPS) * gain
    half = ROPE_DIMS // 2
    return y * cosf + pltpu.roll(y, LANES - half, 1) * sin_lo + pltpu.roll(y, half, 1) * sin_hi


def _mid_kernel(og_ref, x_ref, gate_ref, wo_ref,
                shkv_ref, sckv_ref, nkv_ref, wkv_ref, kg_ref,
                shb_ref, scb_ref, nb_ref, wb_ref, qg_ref,
                cos_ref, slo_ref, shi_ref,
                x1_ref, k_ref, v_ref, kb_ref, vb_ref, q_ref, z_ref, *, n_heads, v_transposed):
    x1 = x_ref[...] + gate_ref[...] * jnp.dot(og_ref[...], wo_ref[...], preferred_element_type=F32)
    x1_ref[...] = x1
    r = x1 * lax.rsqrt(jnp.mean(x1 * x1, axis=-1, keepdims=True) + NORM_EPS)
    cosf, slo, shi = cos_ref[...], slo_ref[...], shi_ref[...]
    width = n_heads * DIFF_VH

    hkv = ((r * nkv_ref[...]) * (1.0 + sckv_ref[...]) + shkv_ref[...]).astype(BF16)
    kraw = jnp.dot(hkv, wkv_ref[:, :width], preferred_element_type=F32)
    for hd in range(n_heads):
        sl = slice(hd * DIFF_VH, (hd + 1) * DIFF_VH)
        kh = _comp_norm_rotary(kraw[:, sl], kg_ref[...], cosf, slo, shi)
        k_ref[:, sl] = kh
        kb_ref[:, sl] = kh.astype(BF16)
    vraw = jnp.dot(hkv, wkv_ref[:, width:], preferred_element_type=F32)
    v_ref[...] = vraw
    if v_transposed:
        for hd in range(n_heads):
            vb_ref[hd] = jnp.transpose(vraw[:, hd * DIFF_VH:(hd + 1) * DIFF_VH]).astype(BF16)
    else:
        vb_ref[...] = vraw.astype(BF16)

    hq = ((r * nb_ref[...]) * (1.0 + scb_ref[...]) + shb_ref[...]).astype(BF16)
    qraw = jnp.dot(hq, wb_ref[:, :width], preferred_element_type=F32)
    for hd in range(n_heads):
        sl = slice(hd * DIFF_VH, (hd + 1) * DIFF_VH)
        qh = _comp_norm_rotary(qraw[:, sl], qg_ref[...], cosf, slo, shi)
        q_ref[:, sl] = (qh * (DIFF_DH ** -0.5 * LOG2_E)).astype(BF16)
    z_ref[...] = _silu(jnp.dot(hq, wb_ref[:, width:], preferred_element_type=F32))


def _mid(og, x, gate, w_out, shift_kv, scale_kv, norm_kv, w_kv, k_gain,
         shift_b, scale_b, norm_b, w_b, q_gain, cosf, sin_lo, sin_hi, *, tm, v_transposed):
    g_, t_, d = x.shape
    v_width = og.shape[-1]
    width = w_kv.shape[1] // 2
    n_heads = width // DIFF_VH
    per_row = gate.shape[1] != 1
    rope_per_group = cosf.shape[0] != 1
    rope_spec = pl.BlockSpec((None, tm, LANES), (lambda g, i: (g, i, 0)) if rope_per_group else (lambda g, i: (0, i, 0)))
    mod = _rows_spec(tm, d, per_row)
    est = ((w_out.size + w_kv.size + w_b.size) * 2 + 2 * tm * (v_width * 2 + d * 4)
           + 2 * tm * width * (4 + 4 + 4 + 2 + 2 + 2 + 4) + 8 * tm * width * 4 + (10 * tm * d * 4 if per_row else 0))
    kern = functools.partial(_mid_kernel, n_heads=n_heads, v_transposed=v_transposed)
    f32_out = jax.ShapeDtypeStruct((g_, t_, width), F32)
    bf_out = jax.ShapeDtypeStruct((g_, t_, width), BF16)
    rows_bf = _rows_spec(tm, width, True)
    if v_transposed:
        vb_out = jax.ShapeDtypeStruct((g_, n_heads, t_ // tm, DIFF_VH, tm), BF16)
        vb_spec = pl.BlockSpec((None, n_heads, None, DIFF_VH, tm), lambda g, i: (g, 0, i, 0, 0))
    else:
        vb_out, vb_spec = bf_out, rows_bf
    return pl.pallas_call(
        kern,
        grid=(g_, t_ // tm),
        in_specs=[_rows_spec(tm, v_width, True), _rows_spec(tm, d, True), mod, _const_spec((v_width, d)),
                  mod, mod, _const_spec((1, d)), _const_spec((d, 2 * width)), _const_spec((1, DIFF_VH)),
                  mod, mod, _const_spec((1, d)), _const_spec((d, 2 * width)), _const_spec((1, DIFF_VH)),
                  rope_spec, rope_spec, rope_spec],
        out_specs=[_rows_spec(tm, d, True), rows_bf, rows_bf, rows_bf, vb_spec, rows_bf, rows_bf],
        out_shape=[jax.ShapeDtypeStruct((g_, t_, d), F32), f32_out, f32_out, bf_out, vb_out, bf_out, f32_out],
        compiler_params=pltpu.CompilerParams(dimension_semantics=("arbitrary", "arbitrary"),
                                             vmem_limit_bytes=_vmem_limit(est)),
        name="mid",
    )(og, x, gate, w_out, shift_kv, scale_kv, norm_kv, w_kv, k_gain,
      shift_b, scale_b, norm_b, w_b, q_gain, cosf, sin_lo, sin_hi)


def _lambda(lam_ref, lam_init):
    lp = lam_ref[...]
    a = jnp.sum(lp[0:1] * lp[1:2], axis=-1, keepdims=True)
    b = jnp.sum(lp[2:3] * lp[3:4], axis=-1, keepdims=True)
    return jnp.exp(a) - jnp.exp(b) + lam_init


def _subln_gate(o, gain, z, lam_init):
    on = o * lax.rsqrt(jnp.mean(o * o, axis=-1, keepdims=True) + NORM_EPS) * gain
    return (on * (1.0 - lam_init)) * z


def _prompt_attn_kernel(q_ref, k_ref, vt_ref, z_ref, lam_ref, gain_ref, o_ref, acc_ref, sa_ref, sb_ref,
                        *, t, nq, lam_init):
    i = pl.program_id(2)

    def components(tile):
        q = q_ref[pl.ds(pl.multiple_of(tile * t, t), t), :]
        lane = lax.broadcasted_iota(jnp.int32, q.shape, 1)
        zero = jnp.zeros_like(q)
        return [jnp.where(lane < DIFF_DH, q, zero), jnp.where(lane < DIFF_DH, zero, q)]

    qc = components(i)
    acc_ref[...] = jnp.zeros_like(acc_ref)
    tv = vt_ref.shape[-1]
    nv = t // tv

    def scores(j, q_parts=qc):
        kj = k_ref[pl.ds(pl.multiple_of(j * t, t), t), :]
        return [lax.dot_general(kj, qq, _NT, preferred_element_type=F32) for qq in q_parts]

    def absorb(j, ss, stats):
        ms, ls = stats[:2], stats[2:]
        m_new = [jnp.maximum(m, jnp.max(sc, axis=0, keepdims=True)) for m, sc in zip(ms, ss)]
        ps = [jnp.exp2(sc - mn) for sc, mn in zip(ss, m_new)]
        alphas = [jnp.exp2(m - mn) for m, mn in zip(ms, m_new)]
        l_new = [a * l + jnp.sum(p, axis=0, keepdims=True) for a, l, p in zip(alphas, ls, ps)]
        p12 = jnp.concatenate([p.astype(BF16) for p in ps], axis=1)
        acc = jnp.concatenate(alphas, axis=1) * acc_ref[...]
        for c in range(nv):
            acc = acc + jnp.dot(vt_ref[j * nv + c], p12[c * tv:(c + 1) * tv], preferred_element_type=F32)
        acc_ref[...] = acc
        return (*m_new, *l_new)

    def put(buf, ss):
        buf[:, :t], buf[:, t:] = ss

    def get(buf):
        return [buf[:, :t], buf[:, t:]]

    neg = jnp.full((1, t), MASK_VALUE, F32)
    zer = jnp.zeros((1, t), F32)

    @pl.when(i == 0)
    def _():
        put(sa_ref, scores(0))

    def produce_and_absorb(cur, nxt, j, stats):
        put(nxt, scores(j + 1))
        return absorb(j, get(cur), stats)

    def body(j, stats):
        return lax.cond(j % 2 == 0,
                        lambda st: produce_and_absorb(sa_ref, sb_ref, j, st),
                        lambda st: produce_and_absorb(sb_ref, sa_ref, j, st), stats)

    stats = lax.fori_loop(0, i, body, (neg, neg, zer, zer))
    keep = lax.broadcasted_iota(jnp.int32, (t, t), 0) <= lax.broadcasted_iota(jnp.int32, (t, t), 1)
    diagonal = lambda buf: (lambda st: absorb(i, [jnp.where(keep, sc, MASK_VALUE) for sc in get(buf)], st))
    _, _, l1, l2 = lax.cond(i % 2 == 0, diagonal(sa_ref), diagonal(sb_ref), stats)
    lam = _lambda(lam_ref, lam_init)
    acc = acc_ref[...]
    ot = acc[:, :t] / l1 - lam * (acc[:, t:] / l2)
    on = ot * lax.rsqrt(jnp.mean(ot * ot, axis=0, keepdims=True) + NORM_EPS) * gain_ref[...]
    o_ref[...] = ((jnp.transpose(on) * (1.0 - lam_init)) * z_ref[...]).astype(BF16)
    put(sa_ref, scores(0, components(jnp.minimum(i + 1, nq - 1))))


def _prompt_attn(q, k, vt, z, lam_params, sub_gain, *, t, lam_init):
    b_, l_, width = q.shape
    n_heads = width // DIFF_VH
    tv = vt.shape[-1]
    kern = functools.partial(_prompt_attn_kernel, t=t, nq=l_ // t, lam_init=lam_init)
    tile = pl.BlockSpec((None, t, DIFF_VH), lambda b, h, i: (b, i, h))
    full = pl.BlockSpec((None, l_, DIFF_VH), lambda b, h, i: (b, 0, h))
    full_t = pl.BlockSpec((None, None, l_ // tv, DIFF_VH, tv), lambda b, h, i: (b, h, 0, 0, 0))
    return pl.pallas_call(
        kern,
        grid=(b_, n_heads, l_ // t),
        in_specs=[full, full, full_t, tile,
                  pl.BlockSpec(lam_params.shape, lambda b, h, i: (0, 0)),
                  pl.BlockSpec((DIFF_VH, 1), lambda b, h, i: (0, 0))],
        out_specs=tile,
        out_shape=jax.ShapeDtypeStruct((b_, l_, width), BF16),
        scratch_shapes=[pltpu.VMEM((DIFF_VH, 2 * t), F32), pltpu.VMEM((t, 2 * t), F32),
                        pltpu.VMEM((t, 2 * t), F32)],
        compiler_params=pltpu.CompilerParams(dimension_semantics=("arbitrary", "arbitrary", "arbitrary")),
        name="prompt_attn",
    )(q, k, vt, z, lam_params, sub_gain)


def _sample_attn_kernel(pt_ref, q_ref, kc_hbm, vc_hbm, kn_ref, vn_ref, z_ref, lam_ref, gain_ref, o_ref,
                        m_ref, l_ref, acc_ref, bias_ref, kbuf, vbuf, sem,
                        *, n_seqs, n_steps, pages_per_step, n_heads, lam_init):
    step = pl.program_id(1)
    g = pl.program_id(0) * n_steps + step
    total = n_seqs * n_steps
    n_slots = kbuf.shape[0]
    ahead = n_slots - 1
    n_rows = q_ref.shape[0]
    rows = bias_ref.shape[1]
    half = n_rows // 2

    def page_copies(gs, slot):
        seq, st = gs // n_steps, gs % n_steps
        out = []
        for p in range(pages_per_step):
            page = pt_ref[seq, st * pages_per_step + p]
            dst = pl.ds(p * rows, rows)
            out.append(pltpu.make_async_copy(kc_hbm.at[page], kbuf.at[slot, dst], sem.at[slot, 0]))
            out.append(pltpu.make_async_copy(vc_hbm.at[page], vbuf.at[slot, dst], sem.at[slot, 1]))
        return out

    @pl.when(g == 0)
    def _():
        for d in range(ahead):
            for cp in page_copies(d, d):
                cp.start()

    @pl.when(g + ahead < total)
    def _():
        for cp in page_copies(g + ahead, (g + ahead) % n_slots):
            cp.start()

    def head_match(n_keys):
        r = lax.broadcasted_iota(jnp.int32, (n_rows, n_keys), 0)
        c = lax.broadcasted_iota(jnp.int32, (n_rows, n_keys), 1)
        return r, c, (r % n_heads) == (c % n_heads)

    @pl.when(step == 0)
    def _():
        m_ref[...] = jnp.full_like(m_ref, MASK_VALUE)
        l_ref[...] = jnp.zeros_like(l_ref)
        acc_ref[...] = jnp.zeros_like(acc_ref)
        _, _, same_head = head_match(rows)
        bias_ref[...] = jnp.where(same_head, 0.0, MASK_VALUE)

    def update(k_all, v_all, biases):
        s_all = lax.dot_general(q_ref[...], k_all.astype(BF16), _NT, preferred_element_type=F32)
        w = s_all.shape[1] // len(biases)
        ss = [s_all[:, n * w:(n + 1) * w] + bias for n, bias in enumerate(biases)]
        m_old = m_ref[...]
        m_new = m_old
        for sb in ss:
            m_new = jnp.maximum(m_new, jnp.max(sb, axis=1, keepdims=True))
        ps = [jnp.exp2(sb - m_new) for sb in ss]
        alpha = jnp.exp2(m_old - m_new)
        l_new = alpha * l_ref[...]
        for pb in ps:
            l_new = l_new + jnp.sum(pb, axis=1, keepdims=True)
        p_all = jnp.concatenate([pb.astype(BF16) for pb in ps], axis=1)
        acc_ref[...] = alpha * acc_ref[...] + jnp.dot(p_all, v_all.astype(BF16), preferred_element_type=F32)
        m_ref[...] = m_new
        l_ref[...] = l_new

    slot = g % n_slots
    for cp in page_copies(g, slot):
        cp.wait()
    update(kbuf[slot], vbuf[slot], [bias_ref[...]] * pages_per_step)

    @pl.when(step == n_steps - 1)
    def _():
        r, c, same_head = head_match(kn_ref.shape[0])
        causal = (c // n_heads) <= ((r % half) // n_heads)
        update(kn_ref[...], vn_ref[...], [jnp.where(same_head & causal, 0.0, MASK_VALUE)])
        lam = _lambda(lam_ref, lam_init)
        of = acc_ref[...] / l_ref[...]
        o = of[:half] - lam * of[half:]
        o_ref[...] = _subln_gate(o, gain_ref[...], z_ref[...], lam_init).astype(BF16)


def _sample_attn(page_table, qmat, cache_k, cache_v, k_new, v_new, z, lam_params, sub_gain, *, lam_init):
    b_, n_rows, _ = qmat.shape
    n_pages = page_table.shape[1]
    rows = cache_k.shape[1]
    n_heads = rows // PAGE_SIZE
    lh = z.shape[1]
    pps = SAMPLE_PAGES_PER_STEP
    n_steps = n_pages // pps
    kern = functools.partial(_sample_attn_kernel, n_seqs=b_, n_steps=n_steps, pages_per_step=pps,
                             n_heads=n_heads, lam_init=lam_init)
    per_seq = lambda r: pl.BlockSpec((None, r, DIFF_VH), lambda b, s, pt: (b, 0, 0))
    hbm = pl.BlockSpec(memory_space=pl.ANY)
    grid_spec = pltpu.PrefetchScalarGridSpec(
        num_scalar_prefetch=1,
        grid=(b_, n_steps),
        in_specs=[per_seq(n_rows), hbm, hbm,
                  per_seq(k_new.shape[1]), per_seq(v_new.shape[1]), per_seq(lh),
                  pl.BlockSpec(lam_params.shape, lambda b, s, pt: (0, 0)),
                  pl.BlockSpec((1, DIFF_VH), lambda b, s, pt: (0, 0))],
        out_specs=per_seq(lh),
        scratch_shapes=[pltpu.VMEM((n_rows, 1), F32), pltpu.VMEM((n_rows, 1), F32),
                        pltpu.VMEM((n_rows, DIFF_VH), F32), pltpu.VMEM((n_rows, rows), F32),
                        pltpu.VMEM((SAMPLE_SLOTS, pps * rows, DIFF_VH), F32),
                        pltpu.VMEM((SAMPLE_SLOTS, pps * rows, DIFF_VH), F32),
                        pltpu.SemaphoreType.DMA((SAMPLE_SLOTS, 2))],
    )
    est = SAMPLE_SLOTS * 2 * pps * rows * DIFF_VH * 4 + (3 * pps + 1) * n_rows * rows * 4
    return pl.pallas_call(
        kern,
        grid_spec=grid_spec,
        out_shape=jax.ShapeDtypeStruct((b_, lh, DIFF_VH), BF16),
        compiler_params=pltpu.CompilerParams(dimension_semantics=("arbitrary", "arbitrary"),
                                             vmem_limit_bytes=_vmem_limit(est)),
        name="sample_attn",
    )(page_table, qmat, cache_k, cache_v, k_new, v_new, z, lam_params, sub_gain)


def _out_kernel(og_ref, x_ref, gate_ref, w_ref, y_ref):
    y_ref[...] = x_ref[...] + gate_ref[...] * jnp.dot(og_ref[...], w_ref[...], preferred_element_type=F32)


def _out_proj(og, x, gate, w, *, tm):
    g_, t_, d = x.shape
    width = og.shape[-1]
    per_row = gate.shape[1] != 1
    return pl.pallas_call(
        _out_kernel,
        grid=(g_, t_ // tm),
        in_specs=[_rows_spec(tm, width, True), _rows_spec(tm, d, True), _rows_spec(tm, d, per_row),
                  _const_spec((width, d))],
        out_specs=_rows_spec(tm, d, True),
        out_shape=jax.ShapeDtypeStruct((g_, t_, d), F32),
        compiler_params=pltpu.CompilerParams(dimension_semantics=("arbitrary", "arbitrary")),
        name="out_proj",
    )(og, x, gate, w)


def _rotary_lane_tables(pos):
    half = ROPE_DIMS // 2
    inv_freq = ROPE_THETA ** (-jnp.arange(0, ROPE_DIMS, 2, dtype=F32) / ROPE_DIMS)
    ang = pos.astype(F32)[:, None] * inv_freq[None, :]
    cos, sin = jnp.cos(ang), jnp.sin(ang)
    n = pos.shape[0]
    rest = DIFF_DH - ROPE_DIMS
    comp = lambda parts: jnp.tile(jnp.concatenate(parts, axis=1), (1, DIFF_VH // DIFF_DH))
    cosf = comp([cos, cos, jnp.ones((n, rest), F32)])
    sin_lo = comp([-sin, jnp.zeros((n, half + rest), F32)])
    sin_hi = comp([jnp.zeros((n, half), F32), sin, jnp.zeros((n, rest), F32)])
    return cosf, sin_lo, sin_hi


def kernel(x_prompt, x_sample, state_gdn, state_conv, cache_k, cache_v, page_table, c_prompt, c_sample,
           ada_w_a, ada_b_a, norm_a, w_in_a, conv_w_a, a_log, dt_bias, gdn_out_gain, w_out_a,
           ada_w_kv, ada_b_kv, norm_kv, w_kv, k_gain,
           ada_w_b, ada_b_b, norm_b, w_in_b, q_gain, lam_params, subln_gain, w_out_b):
    bp, lp, d = x_prompt.shape
    bs, ls, _ = x_sample.shape
    hv = a_log.shape[-1]
    conv_dim = conv_w_a.shape[-1]
    n_heads = w_kv.shape[1] // 2 // DIFF_VH
    width = n_heads * DIFF_VH
    past = page_table.shape[1] * PAGE_SIZE
    lam_init = 0.8 - 0.6 * math.exp(-0.3 * 1)

    n_c = bp + bs
    c_all = jnp.concatenate([c_prompt, c_sample], axis=0)
    pad = -n_c % 16
    c_all = jnp.pad(c_all, ((0, pad), (0, 0))).astype(BF16)
    ada_w = jnp.concatenate([ada_w_a[0], ada_w_kv, ada_w_b[0]], axis=1).astype(BF16)
    ada_b = jnp.concatenate([ada_b_a[0], ada_b_kv, ada_b_b[0]])[None, :]
    mods = _ada(c_all, ada_w, ada_b)
    names = ("shift_a", "scale_a", "gate_a", "shift_kv", "scale_kv", "shift_b", "scale_b", "gate_b")
    mod = {nm: mods[:n_c, i * d:(i + 1) * d] for i, nm in enumerate(names)}

    w_in_bf = w_in_a[0].astype(BF16)
    w_out_a_bf = w_out_a[0].astype(BF16)
    w_kv_bf = w_kv.astype(BF16)
    w_in_b_bf = w_in_b[0].astype(BF16)
    w_out_b_bf = w_out_b[0].astype(BF16)
    norm_a2, norm_kv2, norm_b2 = norm_a[0][None, :], norm_kv[None, :], norm_b[0][None, :]
    k_gain2 = jnp.tile(k_gain, DIFF_VH // DIFF_DH)[None, :]
    q_gain2 = jnp.tile(q_gain[0], DIFF_VH // DIFF_DH)[None, :]
    out_gain2 = gdn_out_gain[0][None, :]
    sub_gain2 = subln_gain[0][None, :]
    lam_p = lam_params[0]
    a_log2, dt_bias2 = a_log[0][None, :], dt_bias[0][None, :]

    pm = {nm: v[:bp][:, None, :] for nm, v in mod.items()}
    tm = min(ROW_TILE, lp)
    conv0 = jnp.zeros((bp, CONV_WIDTH - 1, conv_dim), F32)
    q, k, v, z, g, beta, conv_p = _gdn_in(x_prompt, pm["shift_a"], pm["scale_a"], norm_a2, w_in_bf, conv_w_a[0],
                                          a_log2, dt_bias2, conv0, tm=tm, stride=1)
    s0 = jnp.zeros((bp, hv, GDN_DK, GDN_DV), F32)
    og, st_p = _gdn(q, k, v, z, g, beta, s0, out_gain2)
    rope_p = [t[None] for t in _rotary_lane_tables(jnp.arange(lp, dtype=jnp.int32))]
    x1, k_p, v_p, kb, vb, qb, zb = _mid(og, x_prompt, pm["gate_a"], w_out_a_bf,
                                        pm["shift_kv"], pm["scale_kv"], norm_kv2, w_kv_bf, k_gain2,
                                        pm["shift_b"], pm["scale_b"], norm_b2, w_in_b_bf, q_gain2,
                                        *rope_p, tm=tm, v_transposed=True)
    oa = _prompt_attn(qb, kb, vb, zb, lam_p, subln_gain[0][:, None], t=min(ATTN_TILE, lp), lam_init=lam_init)
    y_prompt = _out_proj(oa, x1, pm["gate_b"], w_out_b_bf, tm=min(OUT_ROW_TILE, lp))

    rows = bs * ls
    tmaj = lambda a: jnp.swapaxes(a, 0, 1).reshape(1, a.shape[0] * a.shape[1], a.shape[2])
    smaj = lambda a: jnp.swapaxes(a.reshape(a.shape[0] // bs, bs, a.shape[1]), 0, 1)
    sm_t = {nm: jnp.tile(v[bp:], (ls, 1))[None] for nm, v in mod.items()}
    conv0_s = tmaj(state_conv[0])
    outs = _gdn_in(tmaj(x_sample), sm_t["shift_a"], sm_t["scale_a"], norm_a2, w_in_bf, conv_w_a[0],
                   a_log2, dt_bias2, conv0_s, tm=rows, stride=bs)
    q, k, v, z, g, beta = [smaj(a[0]) for a in outs[:6]]
    conv_s = smaj(outs[6][0])
    og, st_s = _gdn(q, k, v, z, g, beta, state_gdn[0], out_gain2)
    sm = {nm: jnp.repeat(v[bp:], ls, axis=0)[None] for nm, v in mod.items()}
    pos_s = past + jnp.arange(ls, dtype=jnp.int32)
    rope_s = [jnp.tile(t, (bs, 1))[None] for t in _rotary_lane_tables(pos_s)]
    xs = x_sample.reshape(1, rows, d)
    x1s, k_s, v_s, kbs, vbs, qbs, zbs = _mid(og.reshape(1, rows, -1), xs, sm["gate_a"], w_out_a_bf,
                                             sm["shift_kv"], sm["scale_kv"], norm_kv2, w_kv_bf, k_gain2,
                                             sm["shift_b"], sm["scale_b"], norm_b2, w_in_b_bf, q_gain2,
                                             *rope_s, tm=rows, v_transposed=False)
    q5 = qbs.reshape(bs, ls, n_heads, 2, DIFF_DH)
    qc = jnp.moveaxis(q5, 3, 1)
    zeros = jnp.zeros_like(qc[:, 0])
    qmat = jnp.stack([jnp.concatenate([qc[:, 0], zeros], axis=-1),
                      jnp.concatenate([zeros, qc[:, 1]], axis=-1)], axis=1)
    qmat = qmat.reshape(bs, 2 * ls * n_heads, DIFF_VH)
    tok_head = lambda a: a.reshape(bs, ls * n_heads, DIFF_VH)
    pad_keys = lambda a: jnp.pad(a, ((0, 0), (0, -a.shape[1] % LANES), (0, 0)))
    oas = _sample_attn(page_table, qmat,
                       cache_k.reshape(cache_k.shape[0], PAGE_SIZE * n_heads, DIFF_VH),
                       cache_v.reshape(cache_v.shape[0], PAGE_SIZE * n_heads, DIFF_VH),
                       pad_keys(tok_head(kbs)), pad_keys(tok_head(vbs)), tok_head(zbs), lam_p, sub_gain2,
                       lam_init=lam_init)
    y_sample = _out_proj(oas.reshape(1, rows, width), x1s, sm["gate_b"], w_out_b_bf, tm=rows)

    return (y_prompt, y_sample.reshape(bs, ls, d),
            st_p[None], conv_p[None],
            k_p.reshape(bp, lp, n_heads, DIFF_VH), v_p.reshape(bp, lp, n_heads, DIFF_VH),
            st_s[None], conv_s[None],
            k_s.reshape(bs, ls, n_heads, DIFF_VH), v_s.reshape(bs, ls, n_heads, DIFF_VH))
```

```python
import functools
import math

import jax
import jax.numpy as jnp
from jax import lax
from jax.experimental import pallas as pl
from jax.experimental.pallas import tpu as pltpu

F32 = jnp.float32
BF16 = jnp.bfloat16
NORM_EPS = 1e-6

GDN_DK = 128
GDN_DV = 128
CONV_WIDTH = 4
GDN_CHUNK = 64
TRI_LEAF = 8
GDN_SEQS_PER_STEP = 2
SAMPLE_PAGES_PER_STEP = 8
SAMPLE_SLOTS = 3
ROW_TILE = 256
OUT_ROW_TILE = 512
ATTN_TILE = 512
DIFF_DH = 64
DIFF_VH = 2 * DIFF_DH
ROPE_DIMS = DIFF_DH // 4
ROPE_THETA = 500000.0
PAGE_SIZE = 128
MASK_VALUE = -1e30
LOG2_E = math.log2(math.e)

V7X_VMEM_BYTES = 64 * 1024 * 1024
SUBLANES = 8
LANES = 128

_NT = (((1,), (1,)), ((), ()))
_TN = (((0,), (0,)), ((), ()))


def _vmem_limit(nbytes):
    return int(min(nbytes + (8 << 20), V7X_VMEM_BYTES - (6 << 20)))


def _sigmoid(x):
    return 1.0 / (1.0 + jnp.exp2(x * -LOG2_E))


def _silu(x):
    return x * _sigmoid(x)


def _dot(a, b):
    return jnp.dot(a.astype(BF16), b.astype(BF16), preferred_element_type=F32)


def _dot_nt(a, b):
    return lax.dot_general(a.astype(BF16), b.astype(BF16), _NT, preferred_element_type=F32)


def _dot_tn(a, b):
    return lax.dot_general(a.astype(BF16), b.astype(BF16), _TN, preferred_element_type=F32)


def _rows_spec(tm, width, per_row):
    if per_row:
        return pl.BlockSpec((None, tm, width), lambda g, i: (g, i, 0))
    return pl.BlockSpec((None, 1, width), lambda g, i: (g, 0, 0))


def _const_spec(shape):
    return pl.BlockSpec(shape, lambda g, i: (0,) * len(shape), pipeline_mode=pl.Buffered(1))


def _ada_kernel(c_ref, w_ref, b_ref, o_ref):
    o_ref[...] = jnp.dot(c_ref[...], w_ref[...], preferred_element_type=F32) + b_ref[...]


def _ada(c, w, b):
    rows, d = c.shape
    n = w.shape[1]
    tn = 1024
    return pl.pallas_call(
        _ada_kernel,
        grid=(n // tn,),
        in_specs=[pl.BlockSpec((rows, d), lambda j: (0, 0)),
                  pl.BlockSpec((d, tn), lambda j: (0, j)),
                  pl.BlockSpec((1, tn), lambda j: (0, j))],
        out_specs=pl.BlockSpec((rows, tn), lambda j: (0, j)),
        out_shape=jax.ShapeDtypeStruct((rows, n), F32),
        name="ada",
    )(c, w, b)


def _gdn_in_kernel(x_ref, shift_ref, scale_ref, gain_ref, w_ref, cw_ref, alog_ref, dtb_ref, conv0_ref,
                   q_ref, k_ref, v_ref, z_ref, g_ref, beta_ref, convo_ref, xc_ref,
                   *, tm, stride, nt, qk_width, v_width, n_vheads):
    i = pl.program_id(1)
    conv_dim = 2 * qk_width + v_width
    hist = (CONV_WIDTH - 1) * stride
    hist_base = -(-hist // SUBLANES) * SUBLANES

    x = x_ref[...]
    h = x * lax.rsqrt(jnp.mean(x * x, axis=-1, keepdims=True) + NORM_EPS) * gain_ref[...]
    h = h * (1.0 + scale_ref[...]) + shift_ref[...]
    hb = h.astype(BF16)

    @pl.when(i == 0)
    def _():
        xc_ref[hist_base - hist:hist_base, :] = conv0_ref[...]

    n_qk_heads = qk_width // GDN_DK
    for c0 in range(0, conv_dim, qk_width):
        cols = slice(c0, c0 + qk_width)
        xc_ref[hist_base:hist_base + tm, cols] = jnp.dot(hb, w_ref[:, cols], preferred_element_type=F32)
        if stride % SUBLANES == 0:
            taps = [xc_ref[hist_base - hist + j * stride:hist_base - hist + j * stride + tm, cols]
                    for j in range(CONV_WIDTH)]
        else:
            win = xc_ref[0:hist_base + tm, cols]
            taps = [pltpu.roll(win, (CONV_WIDTH - 1 - j) * stride, 0)[hist_base:] for j in range(CONV_WIDTH - 1)]
            taps.append(win[hist_base:])
        y = taps[0] * cw_ref[0:1, cols]
        for j in range(1, CONV_WIDTH):
            y = y + taps[j] * cw_ref[j:j + 1, cols]
        a = _silu(y)
        if c0 < 2 * qk_width:
            out_ref = q_ref if c0 == 0 else k_ref
            post = GDN_DK ** -0.5 if c0 == 0 else 1.0
            for hd in range(n_qk_heads):
                ah = a[:, hd * GDN_DK:(hd + 1) * GDN_DK]
                nrm = lax.rsqrt(jnp.sum(ah * ah, axis=-1, keepdims=True) + NORM_EPS)
                out_ref[:, hd * GDN_DK:(hd + 1) * GDN_DK] = ah * nrm * post
        else:
            v0 = c0 - 2 * qk_width
            v_ref[:, v0:v0 + qk_width] = a

    new_hist = xc_ref[hist_base + tm - hist:hist_base + tm, :]
    xc_ref[hist_base - hist:hist_base, :] = new_hist

    @pl.when(i == nt - 1)
    def _():
        convo_ref[...] = new_hist

    for c0 in range(0, v_width, qk_width):
        zc = jnp.dot(hb, w_ref[:, conv_dim + c0:conv_dim + c0 + qk_width], preferred_element_type=F32)
        z_ref[:, c0:c0 + qk_width] = _silu(zc)

    g0 = conv_dim + v_width
    bg = jnp.dot(hb, w_ref[:, g0:g0 + 2 * n_vheads], preferred_element_type=F32)
    beta_ref[...] = _sigmoid(bg[:, :n_vheads])
    a_in = bg[:, n_vheads:] + dtb_ref[...]
    softplus = jnp.maximum(a_in, 0.0) + jnp.log1p(jnp.exp(-jnp.abs(a_in)))
    g_ref[...] = -jnp.exp(alog_ref[...]) * softplus


def _gdn_in(x, shift, scale, gain, w_in, conv_w, a_log, dt_bias, conv0, *, tm, stride):
    g_, t_, d = x.shape
    n_vheads = a_log.shape[-1]
    conv_dim = conv_w.shape[-1]
    v_width = n_vheads * GDN_DV
    qk_width = (conv_dim - v_width) // 2
    in_width = w_in.shape[1]
    nt = t_ // tm
    assert t_ % tm == 0 and tm % SUBLANES == 0, (t_, tm)
    hist = (CONV_WIDTH - 1) * stride
    hist_base = -(-hist // SUBLANES) * SUBLANES
    per_row = shift.shape[1] != 1
    kern = functools.partial(_gdn_in_kernel, tm=tm, stride=stride, nt=nt, qk_width=qk_width,
                             v_width=v_width, n_vheads=n_vheads)
    est = (w_in.size * 2 + 2 * tm * d * 4 + 2 * tm * (2 * qk_width + 2 * v_width) * 4
           + (hist_base + tm) * conv_dim * 4 + 4 * tm * qk_width * 4 + (4 * tm * d * 4 if per_row else 0))
    outs = pl.pallas_call(
        kern,
        grid=(g_, nt),
        in_specs=[_rows_spec(tm, d, True), _rows_spec(tm, d, per_row), _rows_spec(tm, d, per_row),
                  _const_spec((1, d)), _const_spec((d, in_width)), _const_spec((CONV_WIDTH, conv_dim)),
                  _const_spec((1, n_vheads)), _const_spec((1, n_vheads)),
                  pl.BlockSpec((None, hist, conv_dim), lambda g, i: (g, 0, 0))],
        out_specs=[_rows_spec(tm, qk_width, True), _rows_spec(tm, qk_width, True),
                   _rows_spec(tm, v_width, True), _rows_spec(tm, v_width, True),
                   _rows_spec(tm, n_vheads, True), _rows_spec(tm, n_vheads, True),
                   pl.BlockSpec((None, hist, conv_dim), lambda g, i: (g, 0, 0))],
        out_shape=[jax.ShapeDtypeStruct((g_, t_, qk_width), F32), jax.ShapeDtypeStruct((g_, t_, qk_width), F32),
                   jax.ShapeDtypeStruct((g_, t_, v_width), F32), jax.ShapeDtypeStruct((g_, t_, v_width), F32),
                   jax.ShapeDtypeStruct((g_, t_, n_vheads), F32), jax.ShapeDtypeStruct((g_, t_, n_vheads), F32),
                   jax.ShapeDtypeStruct((g_, hist, conv_dim), F32)],
        scratch_shapes=[pltpu.VMEM((hist_base + tm, conv_dim), F32)],
        compiler_params=pltpu.CompilerParams(dimension_semantics=("arbitrary", "arbitrary"),
                                             vmem_limit_bytes=_vmem_limit(est)),
        name="gdn_in",
    )(x, shift, scale, gain, w_in, conv_w, a_log, dt_bias, conv0)
    return outs


def _gdn_kernel(q_ref, k_ref, v_ref, z_ref, g_ref, gt_ref, beta_ref, s0_ref, gain_ref,
                og_ref, sout_ref, s_ref, *, c, nc, nb, n_qk_heads, rep):
    n = pl.program_id(1)

    @pl.when(n == 0)
    def _():
        s_ref[...] = s0_ref[...]

    row = lax.broadcasted_iota(jnp.int32, (c, c), 0)
    col = lax.broadcasted_iota(jnp.int32, (c, c), 1)
    lower = row >= col
    strict = row > col
    eye = (row == col).astype(F32)
    leaf = min(c, TRI_LEAF)
    same_block = lambda m: (row >> int(math.log2(m))) == (col >> int(math.log2(m)))
    leaf_mask = same_block(leaf)
    merge_masks = []
    m = leaf
    while m < c:
        merge_masks.append(same_block(2 * m) & jnp.logical_not(same_block(m)))
        m *= 2

    def unit_lower_inverses(a_list):
        ds = [jnp.where(leaf_mask, a, 0.0) for a in a_list]
        ts = [eye - d for d in ds]
        ps = ds
        for _ in range(int(math.log2(leaf)) - 1):
            ps = [_dot(p, p) for p in ps]
            ts = [t + _dot(t, p) for t, p in zip(ts, ps)]
        for mask in merge_masks:
            tmp = [_dot(jnp.where(mask, a, 0.0), t) for a, t in zip(a_list, ts)]
            ts = [t - _dot(t, x) for t, x in zip(ts, tmp)]
        return ts

    seqs = range(nb)
    units = [(bi, hv) for bi in seqs for hv in range(n_qk_heads * rep)]
    qk_units = [(bi, j) for bi in seqs for j in range(n_qk_heads)]
    gc_all = [jnp.dot(lower.astype(F32), g_ref[bi], precision=lax.Precision.HIGHEST,
                      preferred_element_type=F32) for bi in seqs]
    gcr_all = [jnp.dot(gt_ref[bi], (row <= col).astype(F32), precision=lax.Precision.HIGHEST,
                       preferred_element_type=F32) for bi in seqs]
    beta_all = [beta_ref[bi] for bi in seqs]
    qs = {(bi, j): q_ref[bi, :, j * GDN_DK:(j + 1) * GDN_DK] for bi, j in qk_units}
    ks = {(bi, j): k_ref[bi, :, j * GDN_DK:(j + 1) * GDN_DK] for bi, j in qk_units}
    kks = {u: _dot_nt(ks[u], ks[u]) for u in qk_units}
    qks = {u: _dot_nt(qs[u], ks[u]) for u in qk_units}
    qk_of = lambda bi, hv: (bi, hv // rep)
    vcols = lambda hv: slice(hv * GDN_DV, (hv + 1) * GDN_DV)
    gcs = [gc_all[bi][:, hv:hv + 1] for bi, hv in units]
    betas = [beta_all[bi][:, hv:hv + 1] for bi, hv in units]
    decs = [jnp.exp(jnp.where(lower, gc - gcr_all[bi][hv:hv + 1, :], MASK_VALUE)) for (bi, hv), gc in zip(units, gcs)]
    a_list = [jnp.where(strict, kks[qk_of(*u)] * beta * dec, 0.0) for u, beta, dec in zip(units, betas, decs)]
    egs = [jnp.exp(gc) for gc in gcs]
    rhss = [jnp.concatenate([v_ref[bi, :, vcols(hv)] * beta, ks[qk_of(bi, hv)] * (beta * eg)], axis=1)
            for (bi, hv), beta, eg in zip(units, betas, egs)]
    ts = unit_lower_inverses(a_list)
    xss = [_dot(t, rhs) for t, rhs in zip(ts, rhss)]
    ss = [s_ref[bi, hv] for bi, hv in units]
    wqs = [_dot(jnp.concatenate([xs[:, GDN_DV:], qs[qk_of(*u)] * eg], axis=0), s)
           for u, xs, eg, s in zip(units, xss, egs, ss)]
    v_news = [xs[:, :GDN_DV] - wq[:c] for xs, wq in zip(xss, wqs)]
    os_ = [wq[c:] + _dot(qks[qk_of(*u)] * dec, v_new) for u, wq, dec, v_new in zip(units, wqs, decs, v_news)]
    for (bi, hv), gc, s, v_new in zip(units, gcs, ss, v_news):
        g_last = gc[c - 1:c, :]
        s_ref[bi, hv] = s * jnp.exp(g_last) + _dot_tn(ks[qk_of(bi, hv)] * jnp.exp(g_last - gc), v_new)
    for (bi, hv), o in zip(units, os_):
        on = o * lax.rsqrt(jnp.mean(o * o, axis=-1, keepdims=True) + NORM_EPS) * gain_ref[...]
        og_ref[bi, :, vcols(hv)] = (on * z_ref[bi, :, vcols(hv)]).astype(BF16)

    @pl.when(n == nc - 1)
    def _():
        sout_ref[...] = s_ref[...]


def _gdn(q, k, v, z, g, beta, s0, out_gain):
    b_, l_, qk_width = q.shape
    v_width = v.shape[-1]
    hv = g.shape[-1]
    n_qk_heads = qk_width // GDN_DK
    rep = hv // n_qk_heads
    c = min(GDN_CHUNK, l_)
    nc = l_ // c
    assert l_ % c == 0 and c & (c - 1) == 0, (l_, c)
    nb = GDN_SEQS_PER_STEP * (1 if c == GDN_CHUNK else 2)
    nb = nb if b_ % nb == 0 else 1
    gt = jnp.swapaxes(g.reshape(b_, nc, c, hv), 2, 3)
    kern = functools.partial(_gdn_kernel, c=c, nc=nc, nb=nb, n_qk_heads=n_qk_heads, rep=rep)
    row = lambda w: pl.BlockSpec((nb, c, w), lambda b, n: (b, n, 0))
    state = pl.BlockSpec((nb, hv, GDN_DK, GDN_DV), lambda b, n: (b, 0, 0, 0))
    og, s_out = pl.pallas_call(
        kern,
        grid=(b_ // nb, nc),
        in_specs=[row(qk_width), row(qk_width), row(v_width), row(v_width), row(hv),
                  pl.BlockSpec((nb, None, hv, c), lambda b, n: (b, n, 0, 0)), row(hv), state,
                  pl.BlockSpec((1, GDN_DV), lambda b, n: (0, 0))],
        out_specs=[row(v_width), state],
        out_shape=[jax.ShapeDtypeStruct((b_, l_, v_width), BF16),
                   jax.ShapeDtypeStruct((b_, hv, GDN_DK, GDN_DV), F32)],
        scratch_shapes=[pltpu.VMEM((nb, hv, GDN_DK, GDN_DV), F32)],
        compiler_params=pltpu.CompilerParams(dimension_semantics=("arbitrary", "arbitrary")),
        name="gdn",
    )(q, k, v, z, g, gt, beta, s0, out_gain)
    return og, s_out


def _comp_norm_rotary(xh, gain, cosf, sin_lo, sin_hi):
    lane = lax.broadcasted_iota(jnp.int32, xh.shape, 1)
    first = lane < DIFF_DH
    sq = xh * xh
    s_a = jnp.sum(jnp.where(first, sq, 0.0), axis=-1, keepdims=True)
    s_b = jnp.sum(jnp.where(first, 0.0, sq), axis=-1, keepdims=True)
    ms = jnp.where(first, s_a, s_b) * (1.0 / DIFF_DH)
    y = xh * lax.rsqrt(ms + NORM_EPS) * gain
    half = ROPE_DIMS // 2
    return y * cosf + pltpu.roll(y, LANES - half, 1) * sin_lo + pltpu.roll(y, half, 1) * sin_hi


def _mid_kernel(og_ref, x_ref, gate_ref, wo_ref,
                shkv_ref, sckv_ref, nkv_ref, wkv_ref, kg_ref,
                shb_ref, scb_ref, nb_ref, wb_ref, qg_ref,
                cos_ref, slo_ref, shi_ref,
                x1_ref, k_ref, v_ref, kb_ref, vb_ref, q_ref, z_ref, *, n_heads, v_transposed):
    x1 = x_ref[...] + gate_ref[...] * jnp.dot(og_ref[...], wo_ref[...], preferred_element_type=F32)
    x1_ref[...] = x1
    r = x1 * lax.rsqrt(jnp.mean(x1 * x1, axis=-1, keepdims=True) + NORM_EPS)
    cosf, slo, shi = cos_ref[...], slo_ref[...], shi_ref[...]
    width = n_heads * DIFF_VH

    hkv = ((r * nkv_ref[...]) * (1.0 + sckv_ref[...]) + shkv_ref[...]).astype(BF16)
    kraw = jnp.dot(hkv, wkv_ref[:, :width], preferred_element_type=F32)
    for hd in range(n_heads):
        sl = slice(hd * DIFF_VH, (hd + 1) * DIFF_VH)
        kh = _comp_norm_rotary(kraw[:, sl], kg_ref[...], cosf, slo, shi)
        k_ref[:, sl] = kh
        kb_ref[:, sl] = kh.astype(BF16)
    vraw = jnp.dot(hkv, wkv_ref[:, width:], preferred_element_type=F32)
    v_ref[...] = vraw
    if v_transposed:
        for hd in range(n_heads):
            vb_ref[hd] = jnp.transpose(vraw[:, hd * DIFF_VH:(hd + 1) * DIFF_VH]).astype(BF16)
    else:
        vb_ref[...] = vraw.astype(BF16)

    hq = ((r * nb_ref[...]) * (1.0 + scb_ref[...]) + shb_ref[...]).astype(BF16)
    qraw = jnp.dot(hq, wb_ref[:, :width], preferred_element_type=F32)
    for hd in range(n_heads):
        sl = slice(hd * DIFF_VH, (hd + 1) * DIFF_VH)
        qh = _comp_norm_rotary(qraw[:, sl], qg_ref[...], cosf, slo, shi)
        q_ref[:, sl] = (qh * (DIFF_DH ** -0.5 * LOG2_E)).astype(BF16)
    z_ref[...] = _silu(jnp.dot(hq, wb_ref[:, width:], preferred_element_type=F32))


def _mid(og, x, gate, w_out, shift_kv, scale_kv, norm_kv, w_kv, k_gain,
         shift_b, scale_b, norm_b, w_b, q_gain, cosf, sin_lo, sin_hi, *, tm, v_transposed):
    g_, t_, d = x.shape
    v_width = og.shape[-1]
    width = w_kv.shape[1] // 2
    n_heads = width // DIFF_VH
    per_row = gate.shape[1] != 1
    rope_per_group = cosf.shape[0] != 1
    rope_spec = pl.BlockSpec((None, tm, LANES), (lambda g, i: (g, i, 0)) if rope_per_group else (lambda g, i: (0, i, 0)))
    mod = _rows_spec(tm, d, per_row)
    est = ((w_out.size + w_kv.size + w_b.size) * 2 + 2 * tm * (v_width * 2 + d * 4)
           + 2 * tm * width * (4 + 4 + 4 + 2 + 2 + 2 + 4) + 8 * tm * width * 4 + (10 * tm * d * 4 if per_row else 0))
    kern = functools.partial(_mid_kernel, n_heads=n_heads, v_transposed=v_transposed)
    f32_out = jax.ShapeDtypeStruct((g_, t_, width), F32)
    bf_out = jax.ShapeDtypeStruct((g_, t_, width), BF16)
    rows_bf = _rows_spec(tm, width, True)
    if v_transposed:
        vb_out = jax.ShapeDtypeStruct((g_, n_heads, t_ // tm, DIFF_VH, tm), BF16)
        vb_spec = pl.BlockSpec((None, n_heads, None, DIFF_VH, tm), lambda g, i: (g, 0, i, 0, 0))
    else:
        vb_out, vb_spec = bf_out, rows_bf
    return pl.pallas_call(
        kern,
        grid=(g_, t_ // tm),
        in_specs=[_rows_spec(tm, v_width, True), _rows_spec(tm, d, True), mod, _const_spec((v_width, d)),
                  mod, mod, _const_spec((1, d)), _const_spec((d, 2 * width)), _const_spec((1, DIFF_VH)),
                  mod, mod, _const_spec((1, d)), _const_spec((d, 2 * width)), _const_spec((1, DIFF_VH)),
                  rope_spec, rope_spec, rope_spec],
        out_specs=[_rows_spec(tm, d, True), rows_bf, rows_bf, rows_bf, vb_spec, rows_bf, rows_bf],
        out_shape=[jax.ShapeDtypeStruct((g_, t_, d), F32), f32_out, f32_out, bf_out, vb_out, bf_out, f32_out],
        compiler_params=pltpu.CompilerParams(dimension_semantics=("arbitrary", "arbitrary"),
                                             vmem_limit_bytes=_vmem_limit(est)),
        name="mid",
    )(og, x, gate, w_out, shift_kv, scale_kv, norm_kv, w_kv, k_gain,
      shift_b, scale_b, norm_b, w_b, q_gain, cosf, sin_lo, sin_hi)


def _lambda(lam_ref, lam_init):
    lp = lam_ref[...]
    a = jnp.sum(lp[0:1] * lp[1:2], axis=-1, keepdims=True)
    b = jnp.sum(lp[2:3] * lp[3:4], axis=-1, keepdims=True)
    return jnp.exp(a) - jnp.exp(b) + lam_init


def _subln_gate(o, gain, z, lam_init):
    on = o * lax.rsqrt(jnp.mean(o * o, axis=-1, keepdims=True) + NORM_EPS) * gain
    return (on * (1.0 - lam_init)) * z


def _prompt_attn_kernel(q_ref, k_ref, vt_ref, z_ref, lam_ref, gain_ref, o_ref, acc_ref, sa_ref, sb_ref,
                        *, t, nq, lam_init):
    i = pl.program_id(2)

    def components(tile):
        q = q_ref[pl.ds(pl.multiple_of(tile * t, t), t), :]
        lane = lax.broadcasted_iota(jnp.int32, q.shape, 1)
        zero = jnp.zeros_like(q)
        return [jnp.where(lane < DIFF_DH, q, zero), jnp.where(lane < DIFF_DH, zero, q)]

    qc = components(i)
    acc_ref[...] = jnp.zeros_like(acc_ref)
    tv = vt_ref.shape[-1]
    nv = t // tv

    def scores(j, q_parts=qc):
        kj = k_ref[pl.ds(pl.multiple_of(j * t, t), t), :]
        return [lax.dot_general(kj, qq, _NT, preferred_element_type=F32) for qq in q_parts]

    def absorb(j, ss, stats):
        ms, ls = stats[:2], stats[2:]
        m_new = [jnp.maximum(m, jnp.max(sc, axis=0, keepdims=True)) for m, sc in zip(ms, ss)]
        ps = [jnp.exp2(sc - mn) for sc, mn in zip(ss, m_new)]
        alphas = [jnp.exp2(m - mn) for m, mn in zip(ms, m_new)]
        l_new = [a * l + jnp.sum(p, axis=0, keepdims=True) for a, l, p in zip(alphas, ls, ps)]
        p12 = jnp.concatenate([p.astype(BF16) for p in ps], axis=1)
        acc = jnp.concatenate(alphas, axis=1) * acc_ref[...]
        for c in range(nv):
            acc = acc + jnp.dot(vt_ref[j * nv + c], p12[c * tv:(c + 1) * tv], preferred_element_type=F32)
        acc_ref[...] = acc
        return (*m_new, *l_new)

    def put(buf, ss):
        buf[:, :t], buf[:, t:] = ss

    def get(buf):
        return [buf[:, :t], buf[:, t:]]

    neg = jnp.full((1, t), MASK_VALUE, F32)
    zer = jnp.zeros((1, t), F32)

    @pl.when(i == 0)
    def _():
        put(sa_ref, scores(0))

    def produce_and_absorb(cur, nxt, j, stats):
        put(nxt, scores(j + 1))
        return absorb(j, get(cur), stats)

    def body(j, stats):
        return lax.cond(j % 2 == 0,
                        lambda st: produce_and_absorb(sa_ref, sb_ref, j, st),
                        lambda st: produce_and_absorb(sb_ref, sa_ref, j, st), stats)

    stats = lax.fori_loop(0, i, body, (neg, neg, zer, zer))

    keep = lax.broadcasted_iota(jnp.int32, (t, t), 0) <= lax.broadcasted_iota(jnp.int32, (t, t), 1)
    diagonal = lambda buf: (lambda st: absorb(i, [jnp.where(keep, sc, MASK_VALUE) for sc in get(buf)], st))
    _, _, l1, l2 = lax.cond(i % 2 == 0, diagonal(sa_ref), diagonal(sb_ref), stats)
    lam = _lambda(lam_ref, lam_init)
    acc = acc_ref[...]
    ot = acc[:, :t] / l1 - lam * (acc[:, t:] / l2)
    on = ot * lax.rsqrt(jnp.mean(ot * ot, axis=0, keepdims=True) + NORM_EPS) * gain_ref[...]
    o_ref[...] = ((jnp.transpose(on) * (1.0 - lam_init)) * z_ref[...]).astype(BF16)
    put(sa_ref, scores(0, components(jnp.minimum(i + 1, nq - 1))))


def _prompt_attn(q, k, vt, z, lam_params, sub_gain, *, t, lam_init):
    b_, l_, width = q.shape
    n_heads = width // DIFF_VH
    tv = vt.shape[-1]
    assert l_ % t == 0 and t % tv == 0, (l_, t, tv)
    kern = functools.partial(_prompt_attn_kernel, t=t, nq=l_ // t, lam_init=lam_init)
    tile = pl.BlockSpec((None, t, DIFF_VH), lambda b, h, i: (b, i, h))
    full = pl.BlockSpec((None, l_, DIFF_VH), lambda b, h, i: (b, 0, h))
    full_t = pl.BlockSpec((None, None, l_ // tv, DIFF_VH, tv), lambda b, h, i: (b, h, 0, 0, 0))
    return pl.pallas_call(
        kern,
        grid=(b_, n_heads, l_ // t),
        in_specs=[full, full, full_t, tile,
                  pl.BlockSpec(lam_params.shape, lambda b, h, i: (0, 0)),
                  pl.BlockSpec((DIFF_VH, 1), lambda b, h, i: (0, 0))],
        out_specs=tile,
        out_shape=jax.ShapeDtypeStruct((b_, l_, width), BF16),
        scratch_shapes=[pltpu.VMEM((DIFF_VH, 2 * t), F32), pltpu.VMEM((t, 2 * t), F32),
                        pltpu.VMEM((t, 2 * t), F32)],
        compiler_params=pltpu.CompilerParams(dimension_semantics=("arbitrary", "arbitrary", "arbitrary")),
        name="prompt_attn",
    )(q, k, vt, z, lam_params, sub_gain)


def _sample_attn_kernel(pt_ref, q_ref, kc_hbm, vc_hbm, kn_ref, vn_ref, z_ref, lam_ref, gain_ref, o_ref,
                        m_ref, l_ref, acc_ref, bias_ref, kbuf, vbuf, sem,
                        *, n_seqs, n_steps, pages_per_step, n_heads, lam_init):
    step = pl.program_id(1)
    g = pl.program_id(0) * n_steps + step
    total = n_seqs * n_steps
    n_slots = kbuf.shape[0]
    ahead = n_slots - 1
    n_rows = q_ref.shape[0]
    rows = bias_ref.shape[1]
    half = n_rows // 2

    def page_copies(gs, slot):
        seq, st = gs // n_steps, gs % n_steps
        out = []
        for p in range(pages_per_step):
            page = pt_ref[seq, st * pages_per_step + p]
            dst = pl.ds(p * rows, rows)
            out.append(pltpu.make_async_copy(kc_hbm.at[page], kbuf.at[slot, dst], sem.at[slot, 0]))
            out.append(pltpu.make_async_copy(vc_hbm.at[page], vbuf.at[slot, dst], sem.at[slot, 1]))
        return out

    @pl.when(g == 0)
    def _():
        for d in range(ahead):
            for cp in page_copies(d, d):
                cp.start()

    @pl.when(g + ahead < total)
    def _():
        for cp in page_copies(g + ahead, (g + ahead) % n_slots):
            cp.start()

    def head_match(n_keys):
        r = lax.broadcasted_iota(jnp.int32, (n_rows, n_keys), 0)
        c = lax.broadcasted_iota(jnp.int32, (n_rows, n_keys), 1)
        return r, c, (r % n_heads) == (c % n_heads)

    @pl.when(step == 0)
    def _():
        m_ref[...] = jnp.full_like(m_ref, MASK_VALUE)
        l_ref[...] = jnp.zeros_like(l_ref)
        acc_ref[...] = jnp.zeros_like(acc_ref)
        _, _, same_head = head_match(rows)
        bias_ref[...] = jnp.where(same_head, 0.0, MASK_VALUE)

    def update(k_all, v_all, biases):
        s_all = lax.dot_general(q_ref[...], k_all.astype(BF16), _NT, preferred_element_type=F32)
        w = s_all.shape[1] // len(biases)
        ss = [s_all[:, n * w:(n + 1) * w] + bias for n, bias in enumerate(biases)]
        m_old = m_ref[...]
        m_new = m_old
        for sb in ss:
            m_new = jnp.maximum(m_new, jnp.max(sb, axis=1, keepdims=True))
        ps = [jnp.exp2(sb - m_new) for sb in ss]
        alpha = jnp.exp2(m_old - m_new)
        l_new = alpha * l_ref[...]
        for pb in ps:
            l_new = l_new + jnp.sum(pb, axis=1, keepdims=True)
        p_all = jnp.concatenate([pb.astype(BF16) for pb in ps], axis=1)
        acc_ref[...] = alpha * acc_ref[...] + jnp.dot(p_all, v_all.astype(BF16), preferred_element_type=F32)
        m_ref[...] = m_new
        l_ref[...] = l_new

    slot = g % n_slots
    for cp in page_copies(g, slot):
        cp.wait()
    update(kbuf[slot], vbuf[slot], [bias_ref[...]] * pages_per_step)

    @pl.when(step == n_steps - 1)
    def _():
        r, c, same_head = head_match(kn_ref.shape[0])
        causal = (c // n_heads) <= ((r % half) // n_heads)
        update(kn_ref[...], vn_ref[...], [jnp.where(same_head & causal, 0.0, MASK_VALUE)])
        lam = _lambda(lam_ref, lam_init)
        of = acc_ref[...] / l_ref[...]
        o = of[:half] - lam * of[half:]
        o_ref[...] = _subln_gate(o, gain_ref[...], z_ref[...], lam_init).astype(BF16)


def _sample_attn(page_table, qmat, cache_k, cache_v, k_new, v_new, z, lam_params, sub_gain, *, lam_init):
    b_, n_rows, _ = qmat.shape
    n_pages = page_table.shape[1]
    rows = cache_k.shape[1]
    n_heads = rows // PAGE_SIZE
    lh = z.shape[1]
    pps = SAMPLE_PAGES_PER_STEP
    n_steps = n_pages // pps
    assert n_pages % pps == 0 and b_ * n_steps >= SAMPLE_SLOTS, (n_pages, pps, b_)
    kern = functools.partial(_sample_attn_kernel, n_seqs=b_, n_steps=n_steps, pages_per_step=pps,
                             n_heads=n_heads, lam_init=lam_init)
    per_seq = lambda r: pl.BlockSpec((None, r, DIFF_VH), lambda b, s, pt: (b, 0, 0))
    hbm = pl.BlockSpec(memory_space=pl.ANY)
    grid_spec = pltpu.PrefetchScalarGridSpec(
        num_scalar_prefetch=1,
        grid=(b_, n_steps),
        in_specs=[per_seq(n_rows), hbm, hbm,
                  per_seq(k_new.shape[1]), per_seq(v_new.shape[1]), per_seq(lh),
                  pl.BlockSpec(lam_params.shape, lambda b, s, pt: (0, 0)),
                  pl.BlockSpec((1, DIFF_VH), lambda b, s, pt: (0, 0))],
        out_specs=per_seq(lh),
        scratch_shapes=[pltpu.VMEM((n_rows, 1), F32), pltpu.VMEM((n_rows, 1), F32),
                        pltpu.VMEM((n_rows, DIFF_VH), F32), pltpu.VMEM((n_rows, rows), F32),
                        pltpu.VMEM((SAMPLE_SLOTS, pps * rows, DIFF_VH), F32),
                        pltpu.VMEM((SAMPLE_SLOTS, pps * rows, DIFF_VH), F32),
                        pltpu.SemaphoreType.DMA((SAMPLE_SLOTS, 2))],
    )
    est = SAMPLE_SLOTS * 2 * pps * rows * DIFF_VH * 4 + (3 * pps + 1) * n_rows * rows * 4
    return pl.pallas_call(
        kern,
        grid_spec=grid_spec,
        out_shape=jax.ShapeDtypeStruct((b_, lh, DIFF_VH), BF16),
        compiler_params=pltpu.CompilerParams(dimension_semantics=("arbitrary", "arbitrary"),
                                             vmem_limit_bytes=_vmem_limit(est)),
        name="sample_attn",
    )(page_table, qmat, cache_k, cache_v, k_new, v_new, z, lam_params, sub_gain)


def _out_kernel(og_ref, x_ref, gate_ref, w_ref, y_ref):
    y_ref[...] = x_ref[...] + gate_ref[...] * jnp.dot(og_ref[...], w_ref[...], preferred_element_type=F32)


def _out_proj(og, x, gate, w, *, tm):
    g_, t_, d = x.shape
    width = og.shape[-1]
    per_row = gate.shape[1] != 1
    return pl.pallas_call(
        _out_kernel,
        grid=(g_, t_ // tm),
        in_specs=[_rows_spec(tm, width, True), _rows_spec(tm, d, True), _rows_spec(tm, d, per_row),
                  _const_spec((width, d))],
        out_specs=_rows_spec(tm, d, True),
        out_shape=jax.ShapeDtypeStruct((g_, t_, d), F32),
        compiler_params=pltpu.CompilerParams(dimension_semantics=("arbitrary", "arbitrary")),
        name="out_proj",
    )(og, x, gate, w)


def _rotary_lane_tables(pos):
    half = ROPE_DIMS // 2
    inv_freq = ROPE_THETA ** (-jnp.arange(0, ROPE_DIMS, 2, dtype=F32) / ROPE_DIMS)
    ang = pos.astype(F32)[:, None] * inv_freq[None, :]
    cos, sin = jnp.cos(ang), jnp.sin(ang)
    n = pos.shape[0]
    rest = DIFF_DH - ROPE_DIMS
    comp = lambda parts: jnp.tile(jnp.concatenate(parts, axis=1), (1, DIFF_VH // DIFF_DH))
    cosf = comp([cos, cos, jnp.ones((n, rest), F32)])
    sin_lo = comp([-sin, jnp.zeros((n, half + rest), F32)])
    sin_hi = comp([jnp.zeros((n, half), F32), sin, jnp.zeros((n, rest), F32)])
    return cosf, sin_lo, sin_hi


def kernel(x_prompt, x_sample, state_gdn, state_conv, cache_k, cache_v, page_table, c_prompt, c_sample,
           ada_w_a, ada_b_a, norm_a, w_in_a, conv_w_a, a_log, dt_bias, gdn_out_gain, w_out_a,
           ada_w_kv, ada_b_kv, norm_kv, w_kv, k_gain,
           ada_w_b, ada_b_b, norm_b, w_in_b, q_gain, lam_params, subln_gain, w_out_b):
    bp, lp, d = x_prompt.shape
    bs, ls, _ = x_sample.shape
    hv = a_log.shape[-1]
    conv_dim = conv_w_a.shape[-1]
    n_heads = w_kv.shape[1] // 2 // DIFF_VH
    width = n_heads * DIFF_VH
    past = page_table.shape[1] * PAGE_SIZE
    lam_init = 0.8 - 0.6 * math.exp(-0.3 * 1)

    n_c = bp + bs
    c_all = jnp.concatenate([c_prompt, c_sample], axis=0)
    pad = -n_c % 16
    c_all = jnp.pad(c_all, ((0, pad), (0, 0))).astype(BF16)
    ada_w = jnp.concatenate([ada_w_a[0], ada_w_kv, ada_w_b[0]], axis=1).astype(BF16)
    ada_b = jnp.concatenate([ada_b_a[0], ada_b_kv, ada_b_b[0]])[None, :]
    mods = _ada(c_all, ada_w, ada_b)
    names = ("shift_a", "scale_a", "gate_a", "shift_kv", "scale_kv", "shift_b", "scale_b", "gate_b")
    mod = {nm: mods[:n_c, i * d:(i + 1) * d] for i, nm in enumerate(names)}

    w_in_bf = w_in_a[0].astype(BF16)
    w_out_a_bf = w_out_a[0].astype(BF16)
    w_kv_bf = w_kv.astype(BF16)
    w_in_b_bf = w_in_b[0].astype(BF16)
    w_out_b_bf = w_out_b[0].astype(BF16)
    norm_a2, norm_kv2, norm_b2 = norm_a[0][None, :], norm_kv[None, :], norm_b[0][None, :]
    k_gain2 = jnp.tile(k_gain, DIFF_VH // DIFF_DH)[None, :]
    q_gain2 = jnp.tile(q_gain[0], DIFF_VH // DIFF_DH)[None, :]
    out_gain2 = gdn_out_gain[0][None, :]
    sub_gain2 = subln_gain[0][None, :]
    lam_p = lam_params[0]
    a_log2, dt_bias2 = a_log[0][None, :], dt_bias[0][None, :]

    pm = {nm: v[:bp][:, None, :] for nm, v in mod.items()}
    tm = min(ROW_TILE, lp)
    conv0 = jnp.zeros((bp, CONV_WIDTH - 1, conv_dim), F32)
    q, k, v, z, g, beta, conv_p = _gdn_in(x_prompt, pm["shift_a"], pm["scale_a"], norm_a2, w_in_bf, conv_w_a[0],
                                          a_log2, dt_bias2, conv0, tm=tm, stride=1)
    s0 = jnp.zeros((bp, hv, GDN_DK, GDN_DV), F32)
    og, st_p = _gdn(q, k, v, z, g, beta, s0, out_gain2)
    rope_p = [t[None] for t in _rotary_lane_tables(jnp.arange(lp, dtype=jnp.int32))]
    x1, k_p, v_p, kb, vb, qb, zb = _mid(og, x_prompt, pm["gate_a"], w_out_a_bf,
                                        pm["shift_kv"], pm["scale_kv"], norm_kv2, w_kv_bf, k_gain2,
                                        pm["shift_b"], pm["scale_b"], norm_b2, w_in_b_bf, q_gain2,
                                        *rope_p, tm=tm, v_transposed=True)
    oa = _prompt_attn(qb, kb, vb, zb, lam_p, subln_gain[0][:, None], t=min(ATTN_TILE, lp), lam_init=lam_init)
    y_prompt = _out_proj(oa, x1, pm["gate_b"], w_out_b_bf, tm=min(OUT_ROW_TILE, lp))

    rows = bs * ls
    tmaj = lambda a: jnp.swapaxes(a, 0, 1).reshape(1, a.shape[0] * a.shape[1], a.shape[2])
    smaj = lambda a: jnp.swapaxes(a.reshape(a.shape[0] // bs, bs, a.shape[1]), 0, 1)
    sm_t = {nm: jnp.tile(v[bp:], (ls, 1))[None] for nm, v in mod.items()}
    conv0_s = tmaj(state_conv[0])
    outs = _gdn_in(tmaj(x_sample), sm_t["shift_a"], sm_t["scale_a"], norm_a2, w_in_bf, conv_w_a[0],
                   a_log2, dt_bias2, conv0_s, tm=rows, stride=bs)
    q, k, v, z, g, beta = [smaj(a[0]) for a in outs[:6]]
    conv_s = smaj(outs[6][0])
    og, st_s = _gdn(q, k, v, z, g, beta, state_gdn[0], out_gain2)
    sm = {nm: jnp.repeat(v[bp:], ls, axis=0)[None] for nm, v in mod.items()}
    pos_s = past + jnp.arange(ls, dtype=jnp.int32)
    rope_s = [jnp.tile(t, (bs, 1))[None] for t in _rotary_lane_tables(pos_s)]
    xs = x_sample.reshape(1, rows, d)
    x1s, k_s, v_s, kbs, vbs, qbs, zbs = _mid(og.reshape(1, rows, -1), xs, sm["gate_a"], w_out_a_bf,
                                             sm["shift_kv"], sm["scale_kv"], norm_kv2, w_kv_bf, k_gain2,
                                             sm["shift_b"], sm["scale_b"], norm_b2, w_in_b_bf, q_gain2,
                                             *rope_s, tm=rows, v_transposed=False)
    q5 = qbs.reshape(bs, ls, n_heads, 2, DIFF_DH)
    qc = jnp.moveaxis(q5, 3, 1)
    zeros = jnp.zeros_like(qc[:, 0])
    qmat = jnp.stack([jnp.concatenate([qc[:, 0], zeros], axis=-1),
                      jnp.concatenate([zeros, qc[:, 1]], axis=-1)], axis=1)
    qmat = qmat.reshape(bs, 2 * ls * n_heads, DIFF_VH)
    tok_head = lambda a: a.reshape(bs, ls * n_heads, DIFF_VH)
    pad_keys = lambda a: jnp.pad(a, ((0, 0), (0, -a.shape[1] % LANES), (0, 0)))
    oas = _sample_attn(page_table, qmat,
                       cache_k.reshape(cache_k.shape[0], PAGE_SIZE * n_heads, DIFF_VH),
                       cache_v.reshape(cache_v.shape[0], PAGE_SIZE * n_heads, DIFF_VH),
                       pad_keys(tok_head(kbs)), pad_keys(tok_head(vbs)), tok_head(zbs), lam_p, sub_gain2,
                       lam_init=lam_init)
    y_sample = _out_proj(oas.reshape(1, rows, width), x1s, sm["gate_b"], w_out_b_bf, tm=rows)

    return (y_prompt, y_sample.reshape(bs, ls, d),
            st_p[None], conv_p[None],
            k_p.reshape(bp, lp, n_heads, DIFF_VH), v_p.reshape(bp, lp, n_heads, DIFF_VH),
            st_s[None], conv_s[None],
            k_s.reshape(bs, ls, n_heads, DIFF_VH), v_s.reshape(bs, ls, n_heads, DIFF_VH))
```
